```python
import math
import jax, jax.numpy as jnp
from jax import lax
import numpy as np

D_MODEL = 4096
BATCH = 2
SEQ = 4096
DEPTH = 1

CHUNK = 64
PLE_DIM = 256
D_MIX = D_MODEL
HEAD_DIM = 128
ATT_WIDTH = D_MIX // 2
N_ATT_HEADS = ATT_WIDTH // HEAD_DIM
LRU_WIDTH = D_MIX - ATT_WIDTH
N_LRU_BLOCKS = 16
LRU_BLOCK = LRU_WIDTH // N_LRU_BLOCKS
D_IN_PROJ = 3 * ATT_WIDTH + 2 * LRU_WIDTH
REC_CONV = 4
RG_C = 8.0
D_FF = 3 * D_MODEL
FF_CONV = 3
Q_BLOCK = 128
EPS = 1e-6

kernel_name = "hybrid_stickbreak_rglru_convffn_block"


def rmsnorm(x, g):
    xf = x.astype(jnp.float32)
    y = xf * lax.rsqrt(jnp.mean(xf * xf, axis=-1, keepdims=True) + EPS)
    return (y * g.astype(jnp.float32)).astype(x.dtype)


def causal_depthwise_conv(x, w, b):
    K, C = w.shape
    y = lax.conv_general_dilated(
        x, w[:, None, :].astype(x.dtype), window_strides=(1,), padding=[(K - 1, 0)],
        dimension_numbers=("NWC", "WIO", "NWC"), feature_group_count=C)
    return y + b.astype(x.dtype)


def stick_breaking_attention(q, k, v):
    S = q.shape[2]
    q = q * (1.0 / math.sqrt(HEAD_DIM))
    outs = []
    for start in range(0, S, Q_BLOCK):
        end = start + Q_BLOCK
        qb = q[:, :, start:end]
        kb = k[:, :, :end]
        vb = v[:, :, :end]
        z = jnp.einsum("bhqd,bhtd->bhqt", qb, kb).astype(jnp.float32)
        t_idx = start + jnp.arange(Q_BLOCK)
        s_idx = jnp.arange(end)
        mask = s_idx[None, :] < t_idx[:, None]
        log_beta = jax.nn.log_sigmoid(z)
        log_1m = jnp.where(mask, log_beta - z, 0.0)
        suffix = lax.cumsum(log_1m, axis=3, reverse=True) - log_1m
        a = jnp.where(mask, jnp.exp(log_beta + suffix), 0.0)
        outs.append(jnp.einsum("bhqt,bhtd->bhqd", a.astype(vb.dtype), vb))
    return jnp.concatenate(outs, axis=2)


def chunked_linear_scan(a, b):
    B, S, W = a.shape
    n = S // CHUNK
    a = a.reshape(B, n, CHUNK, W)
    b = b.reshape(B, n, CHUNK, W)

    def comb(l, r):
        return (l[0] * r[0], r[0] * l[1] + r[1])

    a_cum, h_loc = lax.associative_scan(comb, (a, b), axis=2)

    def step(carry, inp):
        ac, hc = inp
        h = hc + ac * carry[:, None, :]
        return h[:, -1], h

    _, h = lax.scan(step, jnp.zeros((B, W), jnp.float32),
                    (jnp.swapaxes(a_cum, 0, 1), jnp.swapaxes(h_loc, 0, 1)))
    return jnp.swapaxes(h, 0, 1).reshape(B, S, W)


def rg_lru(xc, w_a, b_a, w_x, b_x, lam):
    B, S, W = xc.shape
    xb = xc.reshape(B, S, N_LRU_BLOCKS, LRU_BLOCK)
    r = jax.nn.sigmoid((jnp.einsum("bsnc,ncd->bsnd", xb, w_a).reshape(B, S, W) + b_a).astype(jnp.float32))
    i = jax.nn.sigmoid((jnp.einsum("bsnc,ncd->bsnd", xb, w_x).reshape(B, S, W) + b_x).astype(jnp.float32))
    log_a = -RG_C * r * jax.nn.softplus(-lam.astype(jnp.float32))
    a = jnp.exp(log_a)
    mult = jnp.sqrt(-jnp.expm1(2.0 * log_a))
    h = chunked_linear_scan(a, mult * (i * xc.astype(jnp.float32)))
    return h.astype(xc.dtype)


def setup_inputs(seed: int = 0) -> dict:
    key = jax.random.key(seed)
    ks = jax.random.split(key, 32)
    f32 = jnp.float32

    def nrm(k, shape, fan_in):
        return jax.random.normal(k, shape, f32) * (fan_in ** -0.5)

    def gain(k, shape):
        return 1.0 + 0.02 * jax.random.normal(k, shape, f32)

    def bias(k, shape):
        return 0.01 * jax.random.normal(k, shape, f32)

    a0 = jax.random.uniform(ks[9], (DEPTH, LRU_WIDTH), f32, 0.9, 0.999) ** (1.0 / RG_C)
    lam = jnp.log(a0) - jnp.log1p(-a0)
    return {
        "x": jax.random.normal(ks[0], (BATCH, SEQ, D_MODEL), f32),
        "p": jax.random.normal(ks[1], (DEPTH, BATCH, SEQ, PLE_DIM), f32),
        "g_mix": gain(ks[2], (DEPTH, D_MODEL)),
        "w_in": nrm(ks[3], (DEPTH, D_MODEL, D_IN_PROJ), D_MODEL),
        "w_rconv": nrm(ks[4], (DEPTH, REC_CONV, LRU_WIDTH), REC_CONV),
        "b_rconv": bias(ks[5], (DEPTH, LRU_WIDTH)),
        "w_rg_a": nrm(ks[6], (DEPTH, N_LRU_BLOCKS, LRU_BLOCK, LRU_BLOCK), LRU_BLOCK),
        "b_rg_a": bias(ks[7], (DEPTH, LRU_WIDTH)),
        "w_rg_x": nrm(ks[8], (DEPTH, N_LRU_BLOCKS, LRU_BLOCK, LRU_BLOCK), LRU_BLOCK),
        "b_rg_x": bias(ks[10], (DEPTH, LRU_WIDTH)),
        "lam": lam,
        "g_att_out": gain(ks[11], (DEPTH, ATT_WIDTH)),
        "g_rec_out": gain(ks[12], (DEPTH, LRU_WIDTH)),
        "w_out": nrm(ks[13], (DEPTH, D_MIX, D_MODEL), D_MIX),
        "g_ffn": gain(ks[14], (DEPTH, D_MODEL)),
        "w_up": nrm(ks[15], (DEPTH, D_MODEL, 2 * D_FF), D_MODEL),
        "w_ffconv": nrm(ks[16], (DEPTH, FF_CONV, 2 * D_FF), FF_CONV),
        "b_ffconv": bias(ks[17], (DEPTH, 2 * D_FF)),
        "w_down": nrm(ks[18], (DEPTH, D_FF, D_MODEL), D_FF),
        "g_ple": gain(ks[19], (DEPTH, D_MODEL)),
        "w_ple": nrm(ks[20], (DEPTH, PLE_DIM, D_MODEL), PLE_DIM),
        "w_ple_gate": nrm(ks[21], (DEPTH, D_MODEL, D_MODEL), D_MODEL),
        "g_final": gain(ks[22], (D_MODEL,)),
    }


def reference(x, p, g_mix, w_in, w_rconv, b_rconv, w_rg_a, b_rg_a, w_rg_x, b_rg_x, lam,
              g_att_out, g_rec_out, w_out, g_ffn, w_up, w_ffconv, b_ffconv, w_down,
              g_ple, w_ple, w_ple_gate, g_final):
    B, S, _ = x.shape
    h = x
    for i in range(DEPTH):
        a_in = rmsnorm(h, g_mix[i])
        proj = a_in @ w_in[i]
        q, k, v, xr, yr = jnp.split(
            proj, [ATT_WIDTH, 2 * ATT_WIDTH, 3 * ATT_WIDTH, 3 * ATT_WIDTH + LRU_WIDTH], axis=-1)
        to_heads = lambda t: t.reshape(B, S, N_ATT_HEADS, HEAD_DIM).transpose(0, 2, 1, 3)
        att = stick_breaking_attention(to_heads(q), to_heads(k), to_heads(v))
        att = att.transpose(0, 2, 1, 3).reshape(B, S, ATT_WIDTH)
        xc = causal_depthwise_conv(xr, w_rconv[i], b_rconv[i])
        rec = jax.nn.gelu(yr) * rg_lru(xc, w_rg_a[i], b_rg_a[i], w_rg_x[i], b_rg_x[i], lam[i])
        mixed = jnp.concatenate([rmsnorm(att, g_att_out[i]), rmsnorm(rec, g_rec_out[i])], axis=-1)
        h = h + mixed @ w_out[i]
        m = rmsnorm(h, g_ffn[i])
        u = causal_depthwise_conv(m @ w_up[i], w_ffconv[i], b_ffconv[i])
        gate, up = jnp.split(u, 2, axis=-1)
        h = h + (jax.nn.gelu(gate) * up) @ w_down[i]
        ple_gate = jax.nn.sigmoid(rmsnorm(h, g_ple[i]) @ w_ple_gate[i])
        h = h + (p[i] @ w_ple[i]) * ple_gate
    return rmsnorm(h, g_final)
```

```python
import functools
import math

import jax
import jax.numpy as jnp
from jax import lax
from jax.experimental import pallas as pl
from jax.experimental.pallas import tpu as pltpu

F32 = jnp.float32
BF16 = jnp.bfloat16

EPS = 1e-6
HEAD_DIM = 128
N_LRU_BLOCKS = 16
RG_C = 8.0
REC_CONV = 4
FF_CONV = 3

V7X_LANES = 128
V7X_SUBLANES = 8
V7X_VMEM_LIMIT_BYTES = 56 * 1024 * 1024

V7X_BF16_ROWS = 2 * V7X_SUBLANES

HALO = V7X_SUBLANES
FFN_HALO = V7X_BF16_ROWS
NORM_ROWS = 64


def _params(semantics, vmem_bytes=V7X_VMEM_LIMIT_BYTES):
    return pltpu.CompilerParams(dimension_semantics=semantics, vmem_limit_bytes=vmem_bytes)


def _rms_rows(x, g):
    ms = jnp.mean(x * x, axis=-1, keepdims=True)
    return x * lax.rsqrt(ms + EPS) * g


def _norm_into(dst_ref, dst_row0, dst_col0, src_ref, g_ref, rows):
    width = src_ref.shape[-1]
    chunk = min(NORM_ROWS, rows)

    def body(c, carry):
        r0 = pl.multiple_of(c * chunk, chunk)
        y = _rms_rows(src_ref[pl.ds(r0, chunk), :], g_ref[...])
        dst_ref[pl.ds(dst_row0 + r0, chunk), pl.ds(dst_col0, width)] = y.astype(dst_ref.dtype)
        return carry

    lax.fori_loop(0, rows // chunk, body, 0)


def _norm_matmul_kernel(x_ref, g_ref, w_ref, cs_ref, o_ref, a_ref):
    @pl.when(pl.program_id(1) == 0)
    def _():
        _norm_into(a_ref, 0, 0, x_ref, g_ref, x_ref.shape[0])

    y = jnp.dot(a_ref[...], w_ref[...], preferred_element_type=F32)
    o_ref[...] = (y * cs_ref[...]).astype(o_ref.dtype)


def _norm_matmul(x, g, w, col_scale, out_dtype, *, tm, tn, name):
    m, d = x.shape
    n = w.shape[1]
    return pl.pallas_call(
        _norm_matmul_kernel,
        grid=(m // tm, n // tn),
        in_specs=[
            pl.BlockSpec((tm, d), lambda i, j: (i, 0)),
            pl.BlockSpec((1, d), lambda i, j: (0, 0)),
            pl.BlockSpec((d, tn), lambda i, j: (0, j)),
            pl.BlockSpec((1, tn), lambda i, j: (0, j)),
        ],
        out_specs=pl.BlockSpec((tm, tn), lambda i, j: (i, j)),
        out_shape=jax.ShapeDtypeStruct((m, n), out_dtype),
        scratch_shapes=[pltpu.VMEM((tm, d), BF16)],
        compiler_params=_params(("parallel", "arbitrary")),
        name=name,
    )(x, g, w, col_scale)


def _attn_block(q, k, v, u_ref, acc_ref, csum_ref, diagonal):
    tq, tk = q.shape[0], k.shape[0]
    z = lax.dot_general(q, k, (((1,), (1,)), ((), ())), preferred_element_type=F32)
    log_beta = jnp.minimum(z, 0.0) - jnp.log1p(jnp.exp(-jnp.abs(z)))
    log_1m = log_beta - z
    if diagonal:
        row = lax.broadcasted_iota(jnp.int32, (tq, tk), 0)
        col = lax.broadcasted_iota(jnp.int32, (tq, tk), 1)
        mask = col < row
        log_1m = jnp.where(mask, log_1m, 0.0)
    hi = log_1m.astype(BF16)
    lo = (log_1m - hi.astype(F32)).astype(BF16)
    u = u_ref[...]
    s = jnp.dot(hi, u, preferred_element_type=F32) + jnp.dot(lo, u, preferred_element_type=F32)
    suffix = s[:, :tk]
    total = s[:, tk:]
    csum = csum_ref[...]
    carry = jnp.concatenate([csum] * (tk // V7X_LANES), axis=1)
    a = jnp.exp(log_beta + suffix + carry)
    if diagonal:
        a = jnp.where(mask, a, 0.0)
    acc_ref[...] += jnp.dot(a.astype(BF16), v, preferred_element_type=F32)
    csum_ref[...] = csum + total


def _attn_kernel(q_ref, k_ref, v_ref, u_ref, o_ref, acc_ref, csum_ref, *, tq, tk):
    qi = pl.program_id(2)
    q = q_ref[...]
    acc_ref[...] = jnp.zeros_like(acc_ref)
    csum_ref[...] = jnp.zeros_like(csum_ref)

    d0 = pl.multiple_of(qi * tq, tq)
    _attn_block(q, k_ref[pl.ds(d0, tk), :], v_ref[pl.ds(d0, tk), :], u_ref, acc_ref, csum_ref, True)

    def body(it, carry):
        k0 = pl.multiple_of((qi - 1 - it) * tk, tk)
        _attn_block(q, k_ref[pl.ds(k0, tk), :], v_ref[pl.ds(k0, tk), :], u_ref, acc_ref, csum_ref,
                    False)
        return carry

    lax.fori_loop(0, qi, body, 0)
    o_ref[...] = acc_ref[...].astype(o_ref.dtype)


def _attention(qkv, batch, seq, n_heads, *, tq):
    tk = tq
    nq = seq // tq
    row = lax.broadcasted_iota(jnp.int32, (tk, tk), 0)
    col = lax.broadcasted_iota(jnp.int32, (tk, tk), 1)
    u = jnp.concatenate(
        [(row > col).astype(BF16), jnp.ones((tk, V7X_LANES), BF16)], axis=1)
    kernel = functools.partial(_attn_kernel, tq=tq, tk=tk)
    return pl.pallas_call(
        kernel,
        grid=(batch, n_heads, nq),
        in_specs=[
            pl.BlockSpec((tq, HEAD_DIM), lambda b, h, i: (b * nq + i, h)),
            pl.BlockSpec((seq, HEAD_DIM), lambda b, h, i: (b, n_heads + h)),
            pl.BlockSpec((seq, HEAD_DIM), lambda b, h, i: (b, 2 * n_heads + h)),
            pl.BlockSpec((tk, tk + V7X_LANES), lambda b, h, i: (0, 0)),
        ],
        out_specs=pl.BlockSpec((tq, HEAD_DIM), lambda b, h, i: (b * nq + i, h)),
        out_shape=jax.ShapeDtypeStruct((batch * seq, n_heads * HEAD_DIM), F32),
        scratch_shapes=[pltpu.VMEM((tq, HEAD_DIM), F32), pltpu.VMEM((tq, V7X_LANES), F32)],
        compiler_params=_params(("parallel", "parallel", "arbitrary")),
        name="stickbreak_attention",
    )(qkv, qkv, qkv, u)


def _softplus(x):
    return jnp.maximum(x, 0.0) + jnp.log1p(jnp.exp(-jnp.abs(x)))


def _rglru_kernel(xr_ref, yr_ref, wc_ref, bc_ref, wa_ref, ba_ref, wx_ref, bx_ref, lam_ref,
                  o_ref, xext_ref, a_ref, b_ref, h_ref, *, ts, tw):
    si = pl.program_id(2)

    @pl.when(si == 0)
    def _():
        xext_ref[pl.ds(0, HALO), :] = jnp.zeros((HALO, tw), F32)
        h_ref[...] = jnp.zeros_like(h_ref)

    @pl.when(si != 0)
    def _():
        xext_ref[pl.ds(0, HALO), :] = xext_ref[pl.ds(ts, HALO), :]

    xext_ref[pl.ds(HALO, ts), :] = xr_ref[...]

    xc = bc_ref[...] + wc_ref[pl.ds(REC_CONV - 1, 1), :] * xr_ref[...]
    for k in range(REC_CONV - 1):
        shift = REC_CONV - 1 - k
        xc = xc + wc_ref[pl.ds(k, 1), :] * xext_ref[pl.ds(HALO - shift, ts), :]

    neg_c_sp = -RG_C * _softplus(-lam_ref[...])
    xc16 = xc.astype(BF16)
    for n in range(tw // V7X_LANES):
        cols = slice(n * V7X_LANES, (n + 1) * V7X_LANES)
        xb = xc16[:, cols]
        r = jax.nn.sigmoid(jnp.dot(xb, wa_ref[n], preferred_element_type=F32) + ba_ref[:, cols])
        i = jax.nn.sigmoid(jnp.dot(xb, wx_ref[n], preferred_element_type=F32) + bx_ref[:, cols])
        log_a = neg_c_sp[:, cols] * r
        a = jnp.exp(log_a)
        mult = jnp.sqrt((1.0 - a) * (1.0 + a))
        a_ref[:, cols] = a
        b_ref[:, cols] = mult * (i * xc[:, cols])

    row = lax.broadcasted_iota(jnp.int32, (V7X_SUBLANES, tw), 0)

    def group(gi, h_prev):
        r0 = pl.multiple_of(gi * V7X_SUBLANES, V7X_SUBLANES)
        a = a_ref[pl.ds(r0, V7X_SUBLANES), :]
        b = b_ref[pl.ds(r0, V7X_SUBLANES), :]
        for d in (1, 2, 4):
            keep = row >= d
            a_sh = jnp.where(keep, pltpu.roll(a, d, 0), 1.0)
            b_sh = jnp.where(keep, pltpu.roll(b, d, 0), 0.0)
            b = a * b_sh + b
            a = a * a_sh
        h = a * h_prev + b
        b_ref[pl.ds(r0, V7X_SUBLANES), :] = h
        return jnp.broadcast_to(h[V7X_SUBLANES - 1:, :], (V7X_SUBLANES, tw))

    h_last = lax.fori_loop(0, ts // V7X_SUBLANES, group, h_ref[...])
    h_ref[...] = h_last
    o_ref[...] = jax.nn.gelu(yr_ref[...]) * b_ref[...]


def _rglru(xy, w_rconv, b_rconv, w_rg_a, b_rg_a, w_rg_x, b_rg_x, lam, batch, seq, *, ts, tw):
    width = lam.shape[-1]
    nw = width // tw
    ns = seq // ts
    gb = tw // V7X_LANES
    row_spec = pl.BlockSpec((1, tw), lambda b, w, s: (0, w))
    kernel = functools.partial(_rglru_kernel, ts=ts, tw=tw)
    return pl.pallas_call(
        kernel,
        grid=(batch, nw, ns),
        in_specs=[
            pl.BlockSpec((ts, tw), lambda b, w, s: (b * ns + s, w)),
            pl.BlockSpec((ts, tw), lambda b, w, s: (b * ns + s, nw + w)),
            pl.BlockSpec((REC_CONV, tw), lambda b, w, s: (0, w)),
            row_spec,
            pl.BlockSpec((gb, V7X_LANES, V7X_LANES), lambda b, w, s: (w, 0, 0)),
            row_spec,
            pl.BlockSpec((gb, V7X_LANES, V7X_LANES), lambda b, w, s: (w, 0, 0)),
            row_spec,
            row_spec,
        ],
        out_specs=pl.BlockSpec((ts, tw), lambda b, w, s: (b * ns + s, w)),
        out_shape=jax.ShapeDtypeStruct((batch * seq, width), F32),
        scratch_shapes=[
            pltpu.VMEM((ts + HALO, tw), F32),
            pltpu.VMEM((ts, tw), F32),
            pltpu.VMEM((ts, tw), F32),
            pltpu.VMEM((V7X_SUBLANES, tw), F32),
        ],
        compiler_params=_params(("parallel", "parallel", "arbitrary")),
        name="rglru_branch",
    )(xy, xy, w_rconv, b_rconv.reshape(1, width), w_rg_a.astype(BF16), b_rg_a.reshape(1, width),
      w_rg_x.astype(BF16), b_rg_x.reshape(1, width), lam.reshape(1, width))


def _outproj_kernel(att_ref, rec_ref, ga_ref, gr_ref, w_ref, res_ref, o_ref, a_ref):
    @pl.when(pl.program_id(1) == 0)
    def _():
        _norm_into(a_ref, 0, 0, att_ref, ga_ref, att_ref.shape[0])
        _norm_into(a_ref, 0, att_ref.shape[1], rec_ref, gr_ref, rec_ref.shape[0])

    o_ref[...] = res_ref[...] + jnp.dot(a_ref[...], w_ref[...], preferred_element_type=F32)


def _outproj(att, rec, g_att, g_rec, w, res, *, tm, tn):
    m, wa = att.shape
    wr = rec.shape[1]
    n = w.shape[1]
    return pl.pallas_call(
        _outproj_kernel,
        grid=(m // tm, n // tn),
        in_specs=[
            pl.BlockSpec((tm, wa), lambda i, j: (i, 0)),
            pl.BlockSpec((tm, wr), lambda i, j: (i, 0)),
            pl.BlockSpec((1, wa), lambda i, j: (0, 0)),
            pl.BlockSpec((1, wr), lambda i, j: (0, 0)),
            pl.BlockSpec((wa + wr, tn), lambda i, j: (0, j)),
            pl.BlockSpec((tm, tn), lambda i, j: (i, j)),
        ],
        out_specs=pl.BlockSpec((tm, tn), lambda i, j: (i, j)),
        out_shape=jax.ShapeDtypeStruct((m, n), F32),
        scratch_shapes=[pltpu.VMEM((tm, wa + wr), BF16)],
        compiler_params=_params(("parallel", "arbitrary")),
        name="out_projection",
    )(att, rec, g_att, g_rec, w, res)


def _ffn_up_kernel(h_ref, halo_ref, g_ref, wg_ref, wu_ref, cg_ref, cu_ref, bg_ref, bu_ref,
                   o_ref, a_ref, *, tm, seq):
    i = pl.program_id(0)

    @pl.when(pl.program_id(1) == 0)
    def _():
        _norm_into(a_ref, FFN_HALO, 0, h_ref, g_ref, tm)
        seq_start = (i * tm) % seq == 0
        halo = _rms_rows(halo_ref[...], g_ref[...])
        a_ref[pl.ds(0, FFN_HALO), :] = jnp.where(seq_start, 0.0, halo).astype(a_ref.dtype)

    a = a_ref[...]

    def conv(w_ref, c_ref, b_ref):
        y = jnp.dot(a, w_ref[...], preferred_element_type=F32)
        out = b_ref[...] + c_ref[pl.ds(FF_CONV - 1, 1), :] * y[FFN_HALO:, :]
        for k in range(FF_CONV - 1):
            shift = FF_CONV - 1 - k
            out = out + c_ref[pl.ds(k, 1), :] * y[FFN_HALO - shift:FFN_HALO - shift + tm, :]
        return out

    gate = conv(wg_ref, cg_ref, bg_ref)
    up = conv(wu_ref, cu_ref, bu_ref)
    o_ref[...] = (jax.nn.gelu(gate) * up).astype(o_ref.dtype)


def _ffn_up(h, g, w_up, w_conv, b_conv, seq, *, tm, tf):
    m, d = h.shape
    f = w_up.shape[1] // 2
    nf = f // tf
    kernel = functools.partial(_ffn_up_kernel, tm=tm, seq=seq)
    halo_blocks = tm // FFN_HALO
    b_conv = b_conv.reshape(1, 2 * f)
    return pl.pallas_call(
        kernel,
        grid=(m // tm, nf),
        in_specs=[
            pl.BlockSpec((tm, d), lambda i, j: (i, 0)),
            pl.BlockSpec((FFN_HALO, d), lambda i, j: (jnp.maximum(i * halo_blocks - 1, 0), 0)),
            pl.BlockSpec((1, d), lambda i, j: (0, 0)),
            pl.BlockSpec((d, tf), lambda i, j: (0, j)),
            pl.BlockSpec((d, tf), lambda i, j: (0, nf + j)),
            pl.BlockSpec((FF_CONV, tf), lambda i, j: (0, j)),
            pl.BlockSpec((FF_CONV, tf), lambda i, j: (0, nf + j)),
            pl.BlockSpec((1, tf), lambda i, j: (0, j)),
            pl.BlockSpec((1, tf), lambda i, j: (0, nf + j)),
        ],
        out_specs=pl.BlockSpec((tm, tf), lambda i, j: (i, j)),
        out_shape=jax.ShapeDtypeStruct((m, f), BF16),
        scratch_shapes=[pltpu.VMEM((FFN_HALO + tm, d), BF16)],
        compiler_params=_params(("parallel", "arbitrary")),
        name="ffn_up_conv_gate",
    )(h, h, g, w_up, w_up, w_conv, w_conv, b_conv, b_conv)


def _matmul_res_kernel(a_ref, w_ref, res_ref, o_ref):
    y = jnp.dot(a_ref[...], w_ref[...], preferred_element_type=F32)

    @pl.when(pl.program_id(2) == 0)
    def _():
        o_ref[...] = res_ref[...] + y

    @pl.when(pl.program_id(2) != 0)
    def _():
        o_ref[...] += y


def _matmul_res(a, w, res, *, tm, tn, tk):
    m, kdim = a.shape
    n = w.shape[1]
    return pl.pallas_call(
        _matmul_res_kernel,
        grid=(m // tm, n // tn, kdim // tk),
        in_specs=[
            pl.BlockSpec((tm, tk), lambda i, j, k: (i, k)),
            pl.BlockSpec((tk, tn), lambda i, j, k: (k, j)),
            pl.BlockSpec((tm, tn), lambda i, j, k: (i, j)),
        ],
        out_specs=pl.BlockSpec((tm, tn), lambda i, j, k: (i, j)),
        out_shape=jax.ShapeDtypeStruct((m, n), F32),
        compiler_params=_params(("parallel", "parallel", "arbitrary")),
        name="ffn_down_projection",
    )(a, w, res)


def _ple_kernel(h_ref, g_ref, wg_ref, p_ref, wp_ref, res_ref, o_ref, a_ref):
    @pl.when(pl.program_id(1) == 0)
    def _():
        _norm_into(a_ref, 0, 0, h_ref, g_ref, h_ref.shape[0])

    gate = jax.nn.sigmoid(jnp.dot(a_ref[...], wg_ref[...], preferred_element_type=F32))
    emb = jnp.dot(p_ref[...].astype(BF16), wp_ref[...], preferred_element_type=F32)
    o_ref[...] = res_ref[...] + emb * gate


def _ple(h, g, w_gate, p, w_ple, *, tm, tn):
    m, d = h.shape
    n = w_gate.shape[1]
    pd = p.shape[1]
    return pl.pallas_call(
        _ple_kernel,
        grid=(m // tm, n // tn),
        in_specs=[
            pl.BlockSpec((tm, d), lambda i, j: (i, 0)),
            pl.BlockSpec((1, d), lambda i, j: (0, 0)),
            pl.BlockSpec((d, tn), lambda i, j: (0, j)),
            pl.BlockSpec((tm, pd), lambda i, j: (i, 0)),
            pl.BlockSpec((pd, tn), lambda i, j: (0, j)),
            pl.BlockSpec((tm, tn), lambda i, j: (i, j)),
        ],
        out_specs=pl.BlockSpec((tm, tn), lambda i, j: (i, j)),
        out_shape=jax.ShapeDtypeStruct((m, n), F32),
        scratch_shapes=[pltpu.VMEM((tm, d), BF16)],
        compiler_params=_params(("parallel", "arbitrary")),
        name="ple_gate",
    )(h, g, w_gate, p, w_ple, h)


def _rmsnorm_kernel(x_ref, g_ref, o_ref):
    o_ref[...] = _rms_rows(x_ref[...], g_ref[...])


def _rmsnorm(x, g, *, tm):
    m, d = x.shape
    return pl.pallas_call(
        _rmsnorm_kernel,
        grid=(m // tm,),
        in_specs=[pl.BlockSpec((tm, d), lambda i: (i, 0)), pl.BlockSpec((1, d), lambda i: (0, 0))],
        out_specs=pl.BlockSpec((tm, d), lambda i: (i, 0)),
        out_shape=jax.ShapeDtypeStruct((m, d), F32),
        compiler_params=_params(("parallel",)),
        name="final_rmsnorm",
    )(x, g)


def kernel(x, p, g_mix, w_in, w_rconv, b_rconv, w_rg_a, b_rg_a, w_rg_x, b_rg_x, lam, g_att_out, g_rec_out, w_out, g_ffn, w_up, w_ffconv, b_ffconv, w_down, g_ple, w_ple, w_ple_gate, g_final):
    batch, seq, d_model = x.shape
    depth = w_in.shape[0]
    lru_width = lam.shape[-1]
    att_width = w_out.shape[1] - lru_width
    n_heads = att_width // HEAD_DIM
    m = batch * seq

    h = x.reshape(m, d_model)
    for l in range(depth):
        w_in_l = w_in[l].astype(BF16)
        qkv_scale = jnp.concatenate(
            [jnp.full((1, att_width), 1.0 / math.sqrt(HEAD_DIM), F32),
             jnp.ones((1, 2 * att_width), F32)], axis=1)
        g_mix_l = g_mix[l].reshape(1, d_model)
        qkv = _norm_matmul(h, g_mix_l, w_in_l[:, :3 * att_width], qkv_scale, BF16,
                           tm=512, tn=1024, name="in_projection_qkv")
        xy = _norm_matmul(h, g_mix_l, w_in_l[:, 3 * att_width:],
                          jnp.ones((1, 2 * lru_width), F32), F32,
                          tm=512, tn=1024, name="in_projection_lru")
        att = _attention(qkv, batch, seq, n_heads, tq=256)
        rec = _rglru(xy, w_rconv[l], b_rconv[l], w_rg_a[l], b_rg_a[l], w_rg_x[l], b_rg_x[l],
                     lam[l], batch, seq, ts=512, tw=512)
        h = _outproj(att, rec, g_att_out[l].reshape(1, att_width),
                     g_rec_out[l].reshape(1, lru_width), w_out[l].astype(BF16), h,
                     tm=512, tn=1024)
        act = _ffn_up(h, g_ffn[l].reshape(1, d_model), w_up[l].astype(BF16), w_ffconv[l],
                      b_ffconv[l], seq, tm=512, tf=512)
        h = _matmul_res(act, w_down[l].astype(BF16), h, tm=1024, tn=1024, tk=2048)
        h = _ple(h, g_ple[l].reshape(1, d_model), w_ple_gate[l].astype(BF16),
                 p[l].reshape(m, -1), w_ple[l].astype(BF16), tm=512, tn=1024)
    out = _rmsnorm(h, g_final.reshape(1, d_model), tm=256)
    return out.reshape(batch, seq, d_model)
```

```python
import functools
import math

import jax
import jax.numpy as jnp
from jax import lax
from jax.experimental import pallas as pl
from jax.experimental.pallas import tpu as pltpu

F32 = jnp.float32
BF16 = jnp.bfloat16

EPS = 1e-6
HEAD_DIM = 128
N_LRU_BLOCKS = 16
RG_C = 8.0
REC_CONV = 4
FF_CONV = 3

V7X_LANES = 128
V7X_SUBLANES = 8
V7X_VMEM_LIMIT_BYTES = 56 * 1024 * 1024

V7X_BF16_ROWS = 2 * V7X_SUBLANES

HALO = V7X_SUBLANES
FFN_HALO = V7X_BF16_ROWS
NORM_ROWS = 64


def _params(semantics, vmem_bytes=V7X_VMEM_LIMIT_BYTES):
    return pltpu.CompilerParams(dimension_semantics=semantics, vmem_limit_bytes=vmem_bytes)


def _rms_rows(x, g):
    ms = jnp.mean(x * x, axis=-1, keepdims=True)
    return x * lax.rsqrt(ms + EPS) * g


def _norm_into(dst_ref, dst_row0, dst_col0, src_ref, g_ref, rows):
    width = src_ref.shape[-1]
    chunk = min(NORM_ROWS, rows)

    def body(c, carry):
        r0 = pl.multiple_of(c * chunk, chunk)
        y = _rms_rows(src_ref[pl.ds(r0, chunk), :], g_ref[...])
        dst_ref[pl.ds(dst_row0 + r0, chunk), pl.ds(dst_col0, width)] = y.astype(dst_ref.dtype)
        return carry

    lax.fori_loop(0, rows // chunk, body, 0)


def _norm_matmul_kernel(x_ref, g_ref, w_ref, cs_ref, o_ref, a_ref):
    @pl.when(pl.program_id(1) == 0)
    def _():
        _norm_into(a_ref, 0, 0, x_ref, g_ref, x_ref.shape[0])

    y = jnp.dot(a_ref[...], w_ref[...], preferred_element_type=F32)
    o_ref[...] = (y * cs_ref[...]).astype(o_ref.dtype)


def _norm_matmul(x, g, w, col_scale, out_dtype, *, tm, tn, name):
    m, d = x.shape
    n = w.shape[1]
    return pl.pallas_call(
        _norm_matmul_kernel,
        grid=(m // tm, n // tn),
        in_specs=[
            pl.BlockSpec((tm, d), lambda i, j: (i, 0)),
            pl.BlockSpec((1, d), lambda i, j: (0, 0)),
            pl.BlockSpec((d, tn), lambda i, j: (0, j)),
            pl.BlockSpec((1, tn), lambda i, j: (0, j)),
        ],
        out_specs=pl.BlockSpec((tm, tn), lambda i, j: (i, j)),
        out_shape=jax.ShapeDtypeStruct((m, n), out_dtype),
        scratch_shapes=[pltpu.VMEM((tm, d), BF16)],
        compiler_params=_params(("parallel", "arbitrary")),
        name=name,
    )(x, g, w, col_scale)


def _attn_tile(q, k, v, negu_ref, acc_ref, csum_ref, cols, diagonal):
    tq, tk = q.shape[0], k.shape[0]
    z = lax.dot_general(q, k, (((1,), (1,)), ((), ())), preferred_element_type=F32)
    sp = jnp.maximum(z, 0.0) + jnp.log(1.0 + jnp.exp(-jnp.abs(z)))
    if diagonal:
        row = lax.broadcasted_iota(jnp.int32, (tq, tk), 0)
        col = lax.broadcasted_iota(jnp.int32, (tq, tk), 1)
        mask = col < row
        sp = jnp.where(mask, sp, 0.0)
    s = jnp.dot(sp.astype(BF16), negu_ref[...], preferred_element_type=F32)
    csum = csum_ref[:, cols]
    a = jnp.exp(z + s + jnp.concatenate([csum] * (tk // V7X_LANES), axis=1))
    if diagonal:
        a = jnp.where(mask, a, 0.0)
    acc_ref[:, cols] += jnp.dot(a.astype(BF16), v, preferred_element_type=F32)
    csum_ref[:, cols] = csum + jnp.broadcast_to(s[:, :1], csum.shape)


def _attn_kernel(q_ref, k_ref, v_ref, negu_ref, o_ref, acc_ref, csum_ref, *, tq, tk, heads):
    qi = pl.program_id(2)
    acc_ref[...] = jnp.zeros_like(acc_ref)
    csum_ref[...] = jnp.zeros_like(csum_ref)

    def key_block(k0, diagonal):
        for g in range(heads):
            cols = slice(g * HEAD_DIM, (g + 1) * HEAD_DIM)
            _attn_tile(q_ref[:, cols], k_ref[pl.ds(k0, tk), cols], v_ref[pl.ds(k0, tk), cols],
                       negu_ref, acc_ref, csum_ref, cols, diagonal)

    key_block(pl.multiple_of(qi * tq, tq), True)

    def body(it, carry):
        key_block(pl.multiple_of((qi - 1 - it) * tk, tk), False)
        return carry

    lax.fori_loop(0, qi, body, 0)
    o_ref[...] = acc_ref[...].astype(o_ref.dtype)


def _attention(qkv, batch, seq, n_heads, *, tq, heads):
    tk = tq
    nq = seq // tq
    width = heads * HEAD_DIM
    groups = n_heads // heads
    row = lax.broadcasted_iota(jnp.int32, (tk, tk), 0)
    col = lax.broadcasted_iota(jnp.int32, (tk, tk), 1)
    negu = -(row >= col).astype(BF16)
    kernel = functools.partial(_attn_kernel, tq=tq, tk=tk, heads=heads)
    return pl.pallas_call(
        kernel,
        grid=(batch, groups, nq),
        in_specs=[
            pl.BlockSpec((tq, width), lambda b, h, i: (b * nq + i, h)),
            pl.BlockSpec((seq, width), lambda b, h, i: (b, groups + h)),
            pl.BlockSpec((seq, width), lambda b, h, i: (b, 2 * groups + h)),
            pl.BlockSpec((tk, tk), lambda b, h, i: (0, 0)),
        ],
        out_specs=pl.BlockSpec((tq, width), lambda b, h, i: (b * nq + i, h)),
        out_shape=jax.ShapeDtypeStruct((batch * seq, n_heads * HEAD_DIM), F32),
        scratch_shapes=[pltpu.VMEM((tq, width), F32), pltpu.VMEM((tq, width), F32)],
        compiler_params=_params(("parallel", "parallel", "arbitrary")),
        name="stickbreak_attention",
    )(qkv, qkv, qkv, negu)


def _softplus(x):
    return jnp.maximum(x, 0.0) + jnp.log1p(jnp.exp(-jnp.abs(x)))


def _rglru_kernel(xr_ref, yr_ref, wc_ref, bc_ref, wa_ref, ba_ref, wx_ref, bx_ref, lam_ref,
                  o_ref, xext_ref, a_ref, b_ref, h_ref, *, ts, tw):
    si = pl.program_id(2)

    @pl.when(si == 0)
    def _():
        xext_ref[pl.ds(0, HALO), :] = jnp.zeros((HALO, tw), F32)
        h_ref[...] = jnp.zeros_like(h_ref)

    @pl.when(si != 0)
    def _():
        xext_ref[pl.ds(0, HALO), :] = xext_ref[pl.ds(ts, HALO), :]

    xext_ref[pl.ds(HALO, ts), :] = xr_ref[...]

    xc = bc_ref[...] + wc_ref[pl.ds(REC_CONV - 1, 1), :] * xr_ref[...]
    for k in range(REC_CONV - 1):
        shift = REC_CONV - 1 - k
        xc = xc + wc_ref[pl.ds(k, 1), :] * xext_ref[pl.ds(HALO - shift, ts), :]

    neg_c_sp = -RG_C * _softplus(-lam_ref[...])
    xc16 = xc.astype(BF16)
    for n in range(tw // V7X_LANES):
        cols = slice(n * V7X_LANES, (n + 1) * V7X_LANES)
        xb = xc16[:, cols]
        r = jax.nn.sigmoid(jnp.dot(xb, wa_ref[n], preferred_element_type=F32) + ba_ref[:, cols])
        i = jax.nn.sigmoid(jnp.dot(xb, wx_ref[n], preferred_element_type=F32) + bx_ref[:, cols])
        log_a = neg_c_sp[:, cols] * r
        a = jnp.exp(log_a)
        mult = jnp.sqrt((1.0 - a) * (1.0 + a))
        a_ref[:, cols] = a
        b_ref[:, cols] = mult * (i * xc[:, cols])

    row = lax.broadcasted_iota(jnp.int32, (V7X_SUBLANES, tw), 0)

    def group(gi, h_prev):
        r0 = pl.multiple_of(gi * V7X_SUBLANES, V7X_SUBLANES)
        a = a_ref[pl.ds(r0, V7X_SUBLANES), :]
        b = b_ref[pl.ds(r0, V7X_SUBLANES), :]
        for d in (1, 2, 4):
            keep = row >= d
            a_sh = jnp.where(keep, pltpu.roll(a, d, 0), 1.0)
            b_sh = jnp.where(keep, pltpu.roll(b, d, 0), 0.0)
            b = a * b_sh + b
            a = a * a_sh
        h = a * h_prev + b
        b_ref[pl.ds(r0, V7X_SUBLANES), :] = h
        return jnp.broadcast_to(h[V7X_SUBLANES - 1:, :], (V7X_SUBLANES, tw))

    h_last = lax.fori_loop(0, ts // V7X_SUBLANES, group, h_ref[...])
    h_ref[...] = h_last
    o_ref[...] = jax.nn.gelu(yr_ref[...]) * b_ref[...]


def _rglru(xy, w_rconv, b_rconv, w_rg_a, b_rg_a, w_rg_x, b_rg_x, lam, batch, seq, *, ts, tw):
    width = lam.shape[-1]
    nw = width // tw
    ns = seq // ts
    gb = tw // V7X_LANES
    row_spec = pl.BlockSpec((1, tw), lambda b, w, s: (0, w))
    kernel = functools.partial(_rglru_kernel, ts=ts, tw=tw)
    return pl.pallas_call(
        kernel,
        grid=(batch, nw, ns),
        in_specs=[
            pl.BlockSpec((ts, tw), lambda b, w, s: (b * ns + s, w)),
            pl.BlockSpec((ts, tw), lambda b, w, s: (b * ns + s, nw + w)),
            pl.BlockSpec((REC_CONV, tw), lambda b, w, s: (0, w)),
            row_spec,
            pl.BlockSpec((gb, V7X_LANES, V7X_LANES), lambda b, w, s: (w, 0, 0)),
            row_spec,
            pl.BlockSpec((gb, V7X_LANES, V7X_LANES), lambda b, w, s: (w, 0, 0)),
            row_spec,
            row_spec,
        ],
        out_specs=pl.BlockSpec((ts, tw), lambda b, w, s: (b * ns + s, w)),
        out_shape=jax.ShapeDtypeStruct((batch * seq, width), F32),
        scratch_shapes=[
            pltpu.VMEM((ts + HALO, tw), F32),
            pltpu.VMEM((ts, tw), F32),
            pltpu.VMEM((ts, tw), F32),
            pltpu.VMEM((V7X_SUBLANES, tw), F32),
        ],
        compiler_params=_params(("parallel", "parallel", "arbitrary")),
        name="rglru_branch",
    )(xy, xy, w_rconv, b_rconv.reshape(1, width), w_rg_a.astype(BF16), b_rg_a.reshape(1, width),
      w_rg_x.astype(BF16), b_rg_x.reshape(1, width), lam.reshape(1, width))


def _outproj_kernel(att_ref, rec_ref, ga_ref, gr_ref, w_ref, res_ref, o_ref, a_ref):
    @pl.when(pl.program_id(1) == 0)
    def _():
        _norm_into(a_ref, 0, 0, att_ref, ga_ref, att_ref.shape[0])
        _norm_into(a_ref, 0, att_ref.shape[1], rec_ref, gr_ref, rec_ref.shape[0])

    o_ref[...] = res_ref[...] + jnp.dot(a_ref[...], w_ref[...], preferred_element_type=F32)


def _outproj(att, rec, g_att, g_rec, w, res, *, tm, tn):
    m, wa = att.shape
    wr = rec.shape[1]
    n = w.shape[1]
    return pl.pallas_call(
        _outproj_kernel,
        grid=(m // tm, n // tn),
        in_specs=[
            pl.BlockSpec((tm, wa), lambda i, j: (i, 0)),
            pl.BlockSpec((tm, wr), lambda i, j: (i, 0)),
            pl.BlockSpec((1, wa), lambda i, j: (0, 0)),
            pl.BlockSpec((1, wr), lambda i, j: (0, 0)),
            pl.BlockSpec((wa + wr, tn), lambda i, j: (0, j)),
            pl.BlockSpec((tm, tn), lambda i, j: (i, j)),
        ],
        out_specs=pl.BlockSpec((tm, tn), lambda i, j: (i, j)),
        out_shape=jax.ShapeDtypeStruct((m, n), F32),
        scratch_shapes=[pltpu.VMEM((tm, wa + wr), BF16)],
        compiler_params=_params(("parallel", "arbitrary")),
        name="out_projection",
    )(att, rec, g_att, g_rec, w, res)


def _ffn_up_kernel(h_ref, halo_ref, g_ref, wg_ref, wu_ref, cg_ref, cu_ref, bg_ref, bu_ref,
                   o_ref, a_ref, *, tm, seq):
    i = pl.program_id(0)

    @pl.when(pl.program_id(1) == 0)
    def _():
        _norm_into(a_ref, FFN_HALO, 0, h_ref, g_ref, tm)
        seq_start = (i * tm) % seq == 0
        halo = _rms_rows(halo_ref[...], g_ref[...])
        a_ref[pl.ds(0, FFN_HALO), :] = jnp.where(seq_start, 0.0, halo).astype(a_ref.dtype)

    a = a_ref[...]

    def conv(w_ref, c_ref, b_ref):
        y = jnp.dot(a, w_ref[...], preferred_element_type=F32)
        out = b_ref[...] + c_ref[pl.ds(FF_CONV - 1, 1), :] * y[FFN_HALO:, :]
        for k in range(FF_CONV - 1):
            shift = FF_CONV - 1 - k
            out = out + c_ref[pl.ds(k, 1), :] * y[FFN_HALO - shift:FFN_HALO - shift + tm, :]
        return out

    gate = conv(wg_ref, cg_ref, bg_ref)
    up = conv(wu_ref, cu_ref, bu_ref)
    o_ref[...] = (jax.nn.gelu(gate) * up).astype(o_ref.dtype)


def _ffn_up(h, g, w_up, w_conv, b_conv, seq, *, tm, tf):
    m, d = h.shape
    f = w_up.shape[1] // 2
    nf = f // tf
    kernel = functools.partial(_ffn_up_kernel, tm=tm, seq=seq)
    halo_blocks = tm // FFN_HALO
    b_conv = b_conv.reshape(1, 2 * f)
    return pl.pallas_call(
        kernel,
        grid=(m // tm, nf),
        in_specs=[
            pl.BlockSpec((tm, d), lambda i, j: (i, 0)),
            pl.BlockSpec((FFN_HALO, d), lambda i, j: (jnp.maximum(i * halo_blocks - 1, 0), 0)),
            pl.BlockSpec((1, d), lambda i, j: (0, 0)),
            pl.BlockSpec((d, tf), lambda i, j: (0, j)),
            pl.BlockSpec((d, tf), lambda i, j: (0, nf + j)),
            pl.BlockSpec((FF_CONV, tf), lambda i, j: (0, j)),
            pl.BlockSpec((FF_CONV, tf), lambda i, j: (0, nf + j)),
            pl.BlockSpec((1, tf), lambda i, j: (0, j)),
            pl.BlockSpec((1, tf), lambda i, j: (0, nf + j)),
        ],
        out_specs=pl.BlockSpec((tm, tf), lambda i, j: (i, j)),
        out_shape=jax.ShapeDtypeStruct((m, f), BF16),
        scratch_shapes=[pltpu.VMEM((FFN_HALO + tm, d), BF16)],
        compiler_params=_params(("parallel", "arbitrary")),
        name="ffn_up_conv_gate",
    )(h, h, g, w_up, w_up, w_conv, w_conv, b_conv, b_conv)


def _matmul_res_kernel(a_ref, w_ref, res_ref, o_ref):
    y = jnp.dot(a_ref[...], w_ref[...], preferred_element_type=F32)

    @pl.when(pl.program_id(2) == 0)
    def _():
        o_ref[...] = res_ref[...] + y

    @pl.when(pl.program_id(2) != 0)
    def _():
        o_ref[...] += y


def _matmul_res(a, w, res, *, tm, tn, tk):
    m, kdim = a.shape
    n = w.shape[1]
    return pl.pallas_call(
        _matmul_res_kernel,
        grid=(m // tm, n // tn, kdim // tk),
        in_specs=[
            pl.BlockSpec((tm, tk), lambda i, j, k: (i, k)),
            pl.BlockSpec((tk, tn), lambda i, j, k: (k, j)),
            pl.BlockSpec((tm, tn), lambda i, j, k: (i, j)),
        ],
        out_specs=pl.BlockSpec((tm, tn), lambda i, j, k: (i, j)),
        out_shape=jax.ShapeDtypeStruct((m, n), F32),
        compiler_params=_params(("parallel", "parallel", "arbitrary")),
        name="ffn_down_projection",
    )(a, w, res)


def _ple_kernel(h_ref, g_ref, wg_ref, p_ref, wp_ref, res_ref, o_ref, a_ref):
    @pl.when(pl.program_id(1) == 0)
    def _():
        _norm_into(a_ref, 0, 0, h_ref, g_ref, h_ref.shape[0])

    gate = jax.nn.sigmoid(jnp.dot(a_ref[...], wg_ref[...], preferred_element_type=F32))
    emb = jnp.dot(p_ref[...].astype(BF16), wp_ref[...], preferred_element_type=F32)
    o_ref[...] = res_ref[...] + emb * gate


def _ple(h, g, w_gate, p, w_ple, *, tm, tn):
    m, d = h.shape
    n = w_gate.shape[1]
    pd = p.shape[1]
    return pl.pallas_call(
        _ple_kernel,
        grid=(m // tm, n // tn),
        in_specs=[
            pl.BlockSpec((tm, d), lambda i, j: (i, 0)),
            pl.BlockSpec((1, d), lambda i, j: (0, 0)),
            pl.BlockSpec((d, tn), lambda i, j: (0, j)),
            pl.BlockSpec((tm, pd), lambda i, j: (i, 0)),
            pl.BlockSpec((pd, tn), lambda i, j: (0, j)),
            pl.BlockSpec((tm, tn), lambda i, j: (i, j)),
        ],
        out_specs=pl.BlockSpec((tm, tn), lambda i, j: (i, j)),
        out_shape=jax.ShapeDtypeStruct((m, n), F32),
        scratch_shapes=[pltpu.VMEM((tm, d), BF16)],
        compiler_params=_params(("parallel", "arbitrary")),
        name="ple_gate",
    )(h, g, w_gate, p, w_ple, h)


def _rmsnorm_kernel(x_ref, g_ref, o_ref):
    o_ref[...] = _rms_rows(x_ref[...], g_ref[...])


def _rmsnorm(x, g, *, tm):
    m, d = x.shape
    return pl.pallas_call(
        _rmsnorm_kernel,
        grid=(m // tm,),
        in_specs=[pl.BlockSpec((tm, d), lambda i: (i, 0)), pl.BlockSpec((1, d), lambda i: (0, 0))],
        out_specs=pl.BlockSpec((tm, d), lambda i: (i, 0)),
        out_shape=jax.ShapeDtypeStruct((m, d), F32),
        compiler_params=_params(("parallel",)),
        name="final_rmsnorm",
    )(x, g)


def kernel(x, p, g_mix, w_in, w_rconv, b_rconv, w_rg_a, b_rg_a, w_rg_x, b_rg_x, lam, g_att_out, g_rec_out, w_out, g_ffn, w_up, w_ffconv, b_ffconv, w_down, g_ple, w_ple, w_ple_gate, g_final):
    batch, seq, d_model = x.shape
    depth = w_in.shape[0]
    lru_width = lam.shape[-1]
    att_width = w_out.shape[1] - lru_width
    n_heads = att_width // HEAD_DIM
    m = batch * seq

    h = x.reshape(m, d_model)
    for l in range(depth):
        w_in_l = w_in[l].astype(BF16)
        qkv_scale = jnp.concatenate(
            [jnp.full((1, att_width), 1.0 / math.sqrt(HEAD_DIM), F32),
             jnp.ones((1, 2 * att_width), F32)], axis=1)
        g_mix_l = g_mix[l].reshape(1, d_model)
        qkv = _norm_matmul(h, g_mix_l, w_in_l[:, :3 * att_width], qkv_scale, BF16,
                           tm=512, tn=1024, name="in_projection_qkv")
        xy = _norm_matmul(h, g_mix_l, w_in_l[:, 3 * att_width:],
                          jnp.ones((1, 2 * lru_width), F32), F32,
                          tm=512, tn=1024, name="in_projection_lru")
        att = _attention(qkv, batch, seq, n_heads, tq=256, heads=4)
        rec = _rglru(xy, w_rconv[l], b_rconv[l], w_rg_a[l], b_rg_a[l], w_rg_x[l], b_rg_x[l],
                     lam[l], batch, seq, ts=512, tw=512)
        h = _outproj(att, rec, g_att_out[l].reshape(1, att_width),
                     g_rec_out[l].reshape(1, lru_width), w_out[l].astype(BF16), h,
                     tm=512, tn=1024)
        act = _ffn_up(h, g_ffn[l].reshape(1, d_model), w_up[l].astype(BF16), w_ffconv[l],
                      b_ffconv[l], seq, tm=512, tf=512)
        h = _matmul_res(act, w_down[l].astype(BF16), h, tm=1024, tn=1024, tk=2048)
        h = _ple(h, g_ple[l].reshape(1, d_model), w_ple_gate[l].astype(BF16),
                 p[l].reshape(m, -1), w_ple[l].astype(BF16), tm=512, tn=1024)
    out = _rmsnorm(h, g_final.reshape(1, d_model), tm=256)
    return out.reshape(batch, seq, d_model)
```

```python
import functools
import math

import jax
import jax.numpy as jnp
from jax import lax
from jax.experimental import pallas as pl
from jax.experimental.pallas import tpu as pltpu

F32 = jnp.float32
BF16 = jnp.bfloat16

EPS = 1e-6
HEAD_DIM = 128
N_LRU_BLOCKS = 16
RG_C = 8.0
REC_CONV = 4
FF_CONV = 3

V7X_LANES = 128
V7X_SUBLANES = 8
V7X_VMEM_LIMIT_BYTES = 60000 * 1024

V7X_BF16_ROWS = 2 * V7X_SUBLANES

HALO = V7X_SUBLANES
FFN_HALO = V7X_BF16_ROWS
NORM_ROWS = 64


def _params(semantics, vmem_bytes=V7X_VMEM_LIMIT_BYTES):
    return pltpu.CompilerParams(dimension_semantics=semantics, vmem_limit_bytes=vmem_bytes)


def _rms_rows(x, g):
    ms = jnp.mean(x * x, axis=-1, keepdims=True)
    return x * lax.rsqrt(ms + EPS) * g


def _norm_into(dst_ref, dst_row0, dst_col0, src_ref, g_ref, rows):
    width = src_ref.shape[-1]
    chunk = min(NORM_ROWS, rows)

    def body(c, carry):
        r0 = pl.multiple_of(c * chunk, chunk)
        y = _rms_rows(src_ref[pl.ds(r0, chunk), :], g_ref[...])
        dst_ref[pl.ds(dst_row0 + r0, chunk), pl.ds(dst_col0, width)] = y.astype(dst_ref.dtype)
        return carry

    lax.fori_loop(0, rows // chunk, body, 0)


def _norm_matmul_kernel(x_ref, g_ref, w_ref, cs_ref, o_ref, a_ref):
    @pl.when(pl.program_id(1) == 0)
    def _():
        _norm_into(a_ref, 0, 0, x_ref, g_ref, x_ref.shape[0])

    y = jnp.dot(a_ref[...], w_ref[...], preferred_element_type=F32)
    o_ref[...] = (y * cs_ref[...]).astype(o_ref.dtype)


def _norm_matmul(x, g, w, col0, n, col_scale, out_dtype, *, tm, tn, name):
    m, d = x.shape
    j0 = col0 // tn
    return pl.pallas_call(
        _norm_matmul_kernel,
        grid=(m // tm, n // tn),
        in_specs=[
            pl.BlockSpec((tm, d), lambda i, j: (i, 0)),
            pl.BlockSpec((1, d), lambda i, j: (0, 0)),
            pl.BlockSpec((d, tn), lambda i, j: (0, j0 + j)),
            pl.BlockSpec((1, tn), lambda i, j: (0, j)),
        ],
        out_specs=pl.BlockSpec((tm, tn), lambda i, j: (i, j)),
        out_shape=jax.ShapeDtypeStruct((m, n), out_dtype),
        scratch_shapes=[pltpu.VMEM((tm, d), BF16)],
        compiler_params=_params(("parallel", "arbitrary")),
        name=name,
    )(x, g, w, col_scale)


MASKED_SCORE = -1e30


def _attn_scores(q, k, diagonal):
    tq, tk = q.shape[0], k.shape[0]
    z = lax.dot_general(q, k, (((1,), (1,)), ((), ())), preferred_element_type=F32)
    sp = jnp.maximum(z, 0.0) + jnp.log(1.0 + jnp.exp(-jnp.abs(z)))
    if diagonal:
        row = lax.broadcasted_iota(jnp.int32, (tq, tk), 0)
        col = lax.broadcasted_iota(jnp.int32, (tq, tk), 1)
        mask = col < row
        sp = jnp.where(mask, sp, 0.0)
        z = jnp.where(mask, z, MASKED_SCORE)
    return z, sp.astype(BF16)


def _attn_accumulate(z, sp, v, negu_ref, acc_ref, csum_ref, cols):
    tk = z.shape[1]
    s = jnp.dot(sp, negu_ref[...], preferred_element_type=F32)
    csum = csum_ref[:, cols]
    a = jnp.exp(z + s + jnp.concatenate([csum] * (tk // V7X_LANES), axis=1))
    acc_ref[:, cols] += jnp.dot(a.astype(BF16), v, preferred_element_type=F32)
    csum_ref[:, cols] = csum + jnp.broadcast_to(s[:, :1], csum.shape)


def _attn_kernel(q_ref, k_ref, v_ref, negu_ref, o_ref, acc_ref, csum_ref, z0_ref, sp0_ref,
                 z1_ref, sp1_ref, *, tq, tk, heads):
    qi = pl.program_id(2)
    acc_ref[...] = jnp.zeros_like(acc_ref)
    csum_ref[...] = jnp.zeros_like(csum_ref)

    def key_rows(n):
        return pl.ds(pl.multiple_of((qi - n) * tk, tk), tk)

    def scores(n, z_ref, sp_ref, diagonal=False):
        rows = key_rows(n)
        for g in range(heads):
            cols = slice(g * HEAD_DIM, (g + 1) * HEAD_DIM)
            z, sp = _attn_scores(q_ref[:, cols], k_ref[rows, cols], diagonal)
            z_ref[:, g * tk:(g + 1) * tk] = z
            sp_ref[:, g * tk:(g + 1) * tk] = sp

    def accumulate(n, z_ref, sp_ref):
        rows = key_rows(n)
        for g in range(heads):
            cols = slice(g * HEAD_DIM, (g + 1) * HEAD_DIM)
            tile = slice(g * tk, (g + 1) * tk)
            _attn_accumulate(z_ref[:, tile], sp_ref[:, tile], v_ref[rows, cols], negu_ref,
                             acc_ref, csum_ref, cols)

    scores(0, z0_ref, sp0_ref, diagonal=True)

    def pair(p, carry):
        n = 2 * p
        accumulate(n, z0_ref, sp0_ref)
        scores(n + 1, z1_ref, sp1_ref)
        accumulate(n + 1, z1_ref, sp1_ref)
        scores(n + 2, z0_ref, sp0_ref)
        return carry

    lax.fori_loop(0, qi // 2, pair, 0)
    n_done = 2 * (qi // 2)

    @pl.when(qi % 2 == 1)
    def _():
        accumulate(n_done, z0_ref, sp0_ref)
        scores(n_done + 1, z1_ref, sp1_ref)
        accumulate(n_done + 1, z1_ref, sp1_ref)

    @pl.when(qi % 2 == 0)
    def _():
        accumulate(n_done, z0_ref, sp0_ref)

    o_ref[...] = acc_ref[...].astype(o_ref.dtype)


def _attention(qkv, batch, seq, n_heads, *, tq, heads):
    tk = tq
    nq = seq // tq
    width = heads * HEAD_DIM
    groups = n_heads // heads
    row = lax.broadcasted_iota(jnp.int32, (tk, tk), 0)
    col = lax.broadcasted_iota(jnp.int32, (tk, tk), 1)
    negu = -(row >= col).astype(BF16)
    kernel = functools.partial(_attn_kernel, tq=tq, tk=tk, heads=heads)
    return pl.pallas_call(
        kernel,
        grid=(batch, groups, nq),
        in_specs=[
            pl.BlockSpec((tq, width), lambda b, h, i: (b * nq + i, h)),
            pl.BlockSpec((seq, width), lambda b, h, i: (b, groups + h)),
            pl.BlockSpec((seq, width), lambda b, h, i: (b, 2 * groups + h)),
            pl.BlockSpec((tk, tk), lambda b, h, i: (0, 0)),
        ],
        out_specs=pl.BlockSpec((tq, width), lambda b, h, i: (b * nq + i, h)),
        out_shape=jax.ShapeDtypeStruct((batch * seq, n_heads * HEAD_DIM), F32),
        scratch_shapes=[
            pltpu.VMEM((tq, width), F32),
            pltpu.VMEM((tq, width), F32),
            pltpu.VMEM((tq, heads * tk), F32),
            pltpu.VMEM((tq, heads * tk), BF16),
            pltpu.VMEM((tq, heads * tk), F32),
            pltpu.VMEM((tq, heads * tk), BF16),
        ],
        compiler_params=_params(("parallel", "parallel", "arbitrary")),
        name="stickbreak_attention",
    )(qkv, qkv, qkv, negu)


def _softplus(x):
    return jnp.maximum(x, 0.0) + jnp.log1p(jnp.exp(-jnp.abs(x)))


def _rglru_kernel(xr_ref, yr_ref, wc_ref, bc_ref, wa_ref, ba_ref, wx_ref, bx_ref, lam_ref,
                  o_ref, xext_ref, a_ref, b_ref, h_ref, *, ts, tw):
    si = pl.program_id(2)

    @pl.when(si == 0)
    def _():
        xext_ref[pl.ds(0, HALO), :] = jnp.zeros((HALO, tw), F32)
        h_ref[...] = jnp.zeros_like(h_ref)

    @pl.when(si != 0)
    def _():
        xext_ref[pl.ds(0, HALO), :] = xext_ref[pl.ds(ts, HALO), :]

    xext_ref[pl.ds(HALO, ts), :] = xr_ref[...]

    xc = bc_ref[...] + wc_ref[pl.ds(REC_CONV - 1, 1), :] * xr_ref[...]
    for k in range(REC_CONV - 1):
        shift = REC_CONV - 1 - k
        xc = xc + wc_ref[pl.ds(k, 1), :] * xext_ref[pl.ds(HALO - shift, ts), :]

    neg_c_sp = -RG_C * _softplus(-lam_ref[...])
    xc16 = xc.astype(BF16)
    for n in range(tw // V7X_LANES):
        cols = slice(n * V7X_LANES, (n + 1) * V7X_LANES)
        xb = xc16[:, cols]
        r = jax.nn.sigmoid(jnp.dot(xb, wa_ref[n], preferred_element_type=F32) + ba_ref[:, cols])
        i = jax.nn.sigmoid(jnp.dot(xb, wx_ref[n], preferred_element_type=F32) + bx_ref[:, cols])
        log_a = neg_c_sp[:, cols] * r
        a = jnp.exp(log_a)
        mult = jnp.sqrt((1.0 - a) * (1.0 + a))
        a_ref[:, cols] = a
        b_ref[:, cols] = mult * (i * xc[:, cols])

    row = lax.broadcasted_iota(jnp.int32, (V7X_SUBLANES, tw), 0)

    def group(gi, h_prev):
        r0 = pl.multiple_of(gi * V7X_SUBLANES, V7X_SUBLANES)
        a = a_ref[pl.ds(r0, V7X_SUBLANES), :]
        b = b_ref[pl.ds(r0, V7X_SUBLANES), :]
        for d in (1, 2, 4):
            keep = row >= d
            a_sh = jnp.where(keep, pltpu.roll(a, d, 0), 1.0)
            b_sh = jnp.where(keep, pltpu.roll(b, d, 0), 0.0)
            b = a * b_sh + b
            a = a * a_sh
        h = a * h_prev + b
        b_ref[pl.ds(r0, V7X_SUBLANES), :] = h
        return jnp.broadcast_to(h[V7X_SUBLANES - 1:, :], (V7X_SUBLANES, tw))

    h_last = lax.fori_loop(0, ts // V7X_SUBLANES, group, h_ref[...])
    h_ref[...] = h_last
    o_ref[...] = jax.nn.gelu(yr_ref[...]) * b_ref[...]


def _rglru(xy, w_rconv, b_rconv, w_rg_a, b_rg_a, w_rg_x, b_rg_x, lam, batch, seq, *, ts, tw):
    width = lam.shape[-1]
    nw = width // tw
    ns = seq // ts
    gb = tw // V7X_LANES
    row_spec = pl.BlockSpec((1, tw), lambda b, w, s: (0, w))
    kernel = functools.partial(_rglru_kernel, ts=ts, tw=tw)
    return pl.pallas_call(
        kernel,
        grid=(batch, nw, ns),
        in_specs=[
            pl.BlockSpec((ts, tw), lambda b, w, s: (b * ns + s, w)),
            pl.BlockSpec((ts, tw), lambda b, w, s: (b * ns + s, nw + w)),
            pl.BlockSpec((REC_CONV, tw), lambda b, w, s: (0, w)),
            row_spec,
            pl.BlockSpec((gb, V7X_LANES, V7X_LANES), lambda b, w, s: (w, 0, 0)),
            row_spec,
            pl.BlockSpec((gb, V7X_LANES, V7X_LANES), lambda b, w, s: (w, 0, 0)),
            row_spec,
            row_spec,
        ],
        out_specs=pl.BlockSpec((ts, tw), lambda b, w, s: (b * ns + s, w)),
        out_shape=jax.ShapeDtypeStruct((batch * seq, width), F32),
        scratch_shapes=[
            pltpu.VMEM((ts + HALO, tw), F32),
            pltpu.VMEM((ts, tw), F32),
            pltpu.VMEM((ts, tw), F32),
            pltpu.VMEM((V7X_SUBLANES, tw), F32),
        ],
        compiler_params=_params(("parallel", "parallel", "arbitrary")),
        name="rglru_branch",
    )(xy, xy, w_rconv, b_rconv.reshape(1, width), w_rg_a.astype(BF16), b_rg_a.reshape(1, width),
      w_rg_x.astype(BF16), b_rg_x.reshape(1, width), lam.reshape(1, width))


def _outproj_kernel(att_ref, rec_ref, ga_ref, gr_ref, w_ref, res_ref, o_ref, a_ref):
    @pl.when(pl.program_id(1) == 0)
    def _():
        _norm_into(a_ref, 0, 0, att_ref, ga_ref, att_ref.shape[0])
        _norm_into(a_ref, 0, att_ref.shape[1], rec_ref, gr_ref, rec_ref.shape[0])

    o_ref[...] = res_ref[...] + jnp.dot(a_ref[...], w_ref[...], preferred_element_type=F32)


def _outproj(att, rec, g_att, g_rec, w, res, *, tm, tn):
    m, wa = att.shape
    wr = rec.shape[1]
    n = w.shape[1]
    return pl.pallas_call(
        _outproj_kernel,
        grid=(m // tm, n // tn),
        in_specs=[
            pl.BlockSpec((tm, wa), lambda i, j: (i, 0)),
            pl.BlockSpec((tm, wr), lambda i, j: (i, 0)),
            pl.BlockSpec((1, wa), lambda i, j: (0, 0)),
            pl.BlockSpec((1, wr), lambda i, j: (0, 0)),
            pl.BlockSpec((wa + wr, tn), lambda i, j: (0, j)),
            pl.BlockSpec((tm, tn), lambda i, j: (i, j)),
        ],
        out_specs=pl.BlockSpec((tm, tn), lambda i, j: (i, j)),
        out_shape=jax.ShapeDtypeStruct((m, n), F32),
        scratch_shapes=[pltpu.VMEM((tm, wa + wr), BF16)],
        compiler_params=_params(("parallel", "arbitrary")),
        name="out_projection",
    )(att, rec, g_att, g_rec, w, res)


def _ffn_up_kernel(h_ref, halo_ref, g_ref, wg_ref, wu_ref, cg_ref, cu_ref, bg_ref, bu_ref,
                   o_ref, a_ref, *, tm, seq):
    i = pl.program_id(0)

    @pl.when(pl.program_id(1) == 0)
    def _():
        _norm_into(a_ref, FFN_HALO, 0, h_ref, g_ref, tm)
        seq_start = (i * tm) % seq == 0
        halo = _rms_rows(halo_ref[...], g_ref[...])
        a_ref[pl.ds(0, FFN_HALO), :] = jnp.where(seq_start, 0.0, halo).astype(a_ref.dtype)

    a = a_ref[...]

    def conv(w_ref, c_ref, b_ref):
        y = jnp.dot(a, w_ref[...], preferred_element_type=F32)
        out = b_ref[...] + c_ref[pl.ds(FF_CONV - 1, 1), :] * y[FFN_HALO:, :]
        for k in range(FF_CONV - 1):
            shift = FF_CONV - 1 - k
            out = out + c_ref[pl.ds(k, 1), :] * y[FFN_HALO - shift:FFN_HALO - shift + tm, :]
        return out

    gate = conv(wg_ref, cg_ref, bg_ref)
    up = conv(wu_ref, cu_ref, bu_ref)
    o_ref[...] = (jax.nn.gelu(gate) * up).astype(o_ref.dtype)


def _ffn_up(h, g, w_up, w_conv, b_conv, seq, *, tm, tf):
    m, d = h.shape
    f = w_up.shape[1] // 2
    nf = f // tf
    kernel = functools.partial(_ffn_up_kernel, tm=tm, seq=seq)
    halo_blocks = tm // FFN_HALO
    b_conv = b_conv.reshape(1, 2 * f)
    return pl.pallas_call(
        kernel,
        grid=(m // tm, nf),
        in_specs=[
            pl.BlockSpec((tm, d), lambda i, j: (i, 0), pipeline_mode=pl.Buffered(1)),
            pl.BlockSpec((FFN_HALO, d), lambda i, j: (jnp.maximum(i * halo_blocks - 1, 0), 0)),
            pl.BlockSpec((1, d), lambda i, j: (0, 0)),
            pl.BlockSpec((d, tf), lambda i, j: (0, j)),
            pl.BlockSpec((d, tf), lambda i, j: (0, nf + j)),
            pl.BlockSpec((FF_CONV, tf), lambda i, j: (0, j)),
            pl.BlockSpec((FF_CONV, tf), lambda i, j: (0, nf + j)),
            pl.BlockSpec((1, tf), lambda i, j: (0, j)),
            pl.BlockSpec((1, tf), lambda i, j: (0, nf + j)),
        ],
        out_specs=pl.BlockSpec((tm, tf), lambda i, j: (i, j)),
        out_shape=jax.ShapeDtypeStruct((m, f), BF16),
        scratch_shapes=[pltpu.VMEM((FFN_HALO + tm, d), BF16)],
        compiler_params=_params(("parallel", "arbitrary")),
        name="ffn_up_conv_gate",
    )(h, h, g, w_up, w_up, w_conv, w_conv, b_conv, b_conv)


def _matmul_res_kernel(a_ref, w_ref, res_ref, o_ref):
    y = jnp.dot(a_ref[...], w_ref[...], preferred_element_type=F32)

    @pl.when(pl.program_id(2) == 0)
    def _():
        o_ref[...] = res_ref[...] + y

    @pl.when(pl.program_id(2) != 0)
    def _():
        o_ref[...] += y


def _matmul_res(a, w, res, *, tm, tn, tk):
    m, kdim = a.shape
    n = w.shape[1]
    return pl.pallas_call(
        _matmul_res_kernel,
        grid=(m // tm, n // tn, kdim // tk),
        in_specs=[
            pl.BlockSpec((tm, tk), lambda i, j, k: (i, k)),
            pl.BlockSpec((tk, tn), lambda i, j, k: (k, j)),
            pl.BlockSpec((tm, tn), lambda i, j, k: (i, j)),
        ],
        out_specs=pl.BlockSpec((tm, tn), lambda i, j, k: (i, j)),
        out_shape=jax.ShapeDtypeStruct((m, n), F32),
        compiler_params=_params(("parallel", "parallel", "arbitrary")),
        name="ffn_down_projection",
    )(a, w, res)


def _ple_kernel(h_ref, g_ref, wg_ref, p_ref, wp_ref, res_ref, o_ref, a_ref):
    @pl.when(pl.program_id(1) == 0)
    def _():
        _norm_into(a_ref, 0, 0, h_ref, g_ref, h_ref.shape[0])

    gate = jax.nn.sigmoid(jnp.dot(a_ref[...], wg_ref[...], preferred_element_type=F32))
    emb = jnp.dot(p_ref[...].astype(BF16), wp_ref[...], preferred_element_type=F32)
    o_ref[...] = res_ref[...] + emb * gate


def _ple(h, g, w_gate, p, w_ple, *, tm, tn):
    m, d = h.shape
    n = w_gate.shape[1]
    pd = p.shape[1]
    return pl.pallas_call(
        _ple_kernel,
        grid=(m // tm, n // tn),
        in_specs=[
            pl.BlockSpec((tm, d), lambda i, j: (i, 0)),
            pl.BlockSpec((1, d), lambda i, j: (0, 0)),
            pl.BlockSpec((d, tn), lambda i, j: (0, j)),
            pl.BlockSpec((tm, pd), lambda i, j: (i, 0)),
            pl.BlockSpec((pd, tn), lambda i, j: (0, j)),
            pl.BlockSpec((tm, tn), lambda i, j: (i, j)),
        ],
        out_specs=pl.BlockSpec((tm, tn), lambda i, j: (i, j)),
        out_shape=jax.ShapeDtypeStruct((m, n), F32),
        scratch_shapes=[pltpu.VMEM((tm, d), BF16)],
        compiler_params=_params(("parallel", "arbitrary")),
        name="ple_gate",
    )(h, g, w_gate, p, w_ple, h)


def _rmsnorm_kernel(x_ref, g_ref, o_ref):
    o_ref[...] = _rms_rows(x_ref[...], g_ref[...])


def _rmsnorm(x, g, *, tm):
    m, d = x.shape
    return pl.pallas_call(
        _rmsnorm_kernel,
        grid=(m // tm,),
        in_specs=[pl.BlockSpec((tm, d), lambda i: (i, 0)), pl.BlockSpec((1, d), lambda i: (0, 0))],
        out_specs=pl.BlockSpec((tm, d), lambda i: (i, 0)),
        out_shape=jax.ShapeDtypeStruct((m, d), F32),
        compiler_params=_params(("parallel",)),
        name="final_rmsnorm",
    )(x, g)


def kernel(x, p, g_mix, w_in, w_rconv, b_rconv, w_rg_a, b_rg_a, w_rg_x, b_rg_x, lam, g_att_out, g_rec_out, w_out, g_ffn, w_up, w_ffconv, b_ffconv, w_down, g_ple, w_ple, w_ple_gate, g_final):
    batch, seq, d_model = x.shape
    depth = w_in.shape[0]
    lru_width = lam.shape[-1]
    att_width = w_out.shape[1] - lru_width
    n_heads = att_width // HEAD_DIM
    m = batch * seq

    h = x.reshape(m, d_model)
    for l in range(depth):
        w_in_l = w_in[l].astype(BF16)
        qkv_scale = jnp.concatenate(
            [jnp.full((1, att_width), 1.0 / math.sqrt(HEAD_DIM), F32),
             jnp.ones((1, 2 * att_width), F32)], axis=1)
        g_mix_l = g_mix[l].reshape(1, d_model)
        qkv = _norm_matmul(h, g_mix_l, w_in_l, 0, 3 * att_width, qkv_scale, BF16,
                           tm=512, tn=1024, name="in_projection_qkv")
        xy = _norm_matmul(h, g_mix_l, w_in_l, 3 * att_width, 2 * lru_width,
                          jnp.ones((1, 2 * lru_width), F32), F32,
                          tm=512, tn=1024, name="in_projection_lru")
        att = _attention(qkv, batch, seq, n_heads, tq=256, heads=4)
        rec = _rglru(xy, w_rconv[l], b_rconv[l], w_rg_a[l], b_rg_a[l], w_rg_x[l], b_rg_x[l],
                     lam[l], batch, seq, ts=512, tw=512)
        h = _outproj(att, rec, g_att_out[l].reshape(1, att_width),
                     g_rec_out[l].reshape(1, lru_width), w_out[l].astype(BF16), h,
                     tm=512, tn=1024)
        act = _ffn_up(h, g_ffn[l].reshape(1, d_model), w_up[l].astype(BF16), w_ffconv[l],
                      b_ffconv[l], seq, tm=1024, tf=512)
        h = _matmul_res(act, w_down[l].astype(BF16), h, tm=1024, tn=1024, tk=4096)
        h = _ple(h, g_ple[l].reshape(1, d_model), w_ple_gate[l].astype(BF16),
                 p[l].reshape(m, -1), w_ple[l].astype(BF16), tm=512, tn=1024)
    out = _rmsnorm(h, g_final.reshape(1, d_model), tm=256)
    return out.reshape(batch, seq, d_model)
```

```python
import functools
import math

import jax
import jax.numpy as jnp
from jax import lax
from jax.experimental import pallas as pl
from jax.experimental.pallas import tpu as pltpu

F32 = jnp.float32
BF16 = jnp.bfloat16

EPS = 1e-6
HEAD_DIM = 128
N_LRU_BLOCKS = 16
RG_C = 8.0
REC_CONV = 4
FF_CONV = 3

V7X_LANES = 128
V7X_SUBLANES = 8
V7X_VMEM_LIMIT_BYTES = 60000 * 1024

V7X_BF16_ROWS = 2 * V7X_SUBLANES

HALO = V7X_SUBLANES
FFN_HALO = V7X_BF16_ROWS
NORM_ROWS = 32


def _params(semantics, vmem_bytes=V7X_VMEM_LIMIT_BYTES):
    return pltpu.CompilerParams(dimension_semantics=semantics, vmem_limit_bytes=vmem_bytes)


def _rms_rows(x, g):
    ms = jnp.mean(x * x, axis=-1, keepdims=True)
    return x * lax.rsqrt(ms + EPS) * g


def _norm_into(dst_ref, dst_row0, dst_col0, src_ref, g_ref, rows, stat_ref):
    width = src_ref.shape[-1]
    lane_tiles = width // V7X_LANES
    chunk = min(NORM_ROWS, rows)

    def sum_squares(c, carry):
        r0 = pl.multiple_of(c * chunk, chunk)
        x = src_ref[pl.ds(r0, chunk), :]
        sq = x * x
        parts = [sq[:, k * V7X_LANES:(k + 1) * V7X_LANES] for k in range(lane_tiles)]
        while len(parts) > 1:
            parts = [a + b for a, b in zip(parts[0::2], parts[1::2])] + parts[len(parts) & ~1:]
        stat_ref[pl.ds(r0, chunk), :] = parts[0]
        return carry

    lax.fori_loop(0, rows // chunk, sum_squares, 0)
    ms = jnp.sum(stat_ref[pl.ds(0, rows), :], axis=-1, keepdims=True) * (1.0 / width)
    stat_ref[pl.ds(0, rows), :] = jnp.broadcast_to(lax.rsqrt(ms + EPS), (rows, V7X_LANES))

    def scale(c, carry):
        r0 = pl.multiple_of(c * chunk, chunk)
        rstd = jnp.concatenate([stat_ref[pl.ds(r0, chunk), :]] * lane_tiles, axis=1)
        y = src_ref[pl.ds(r0, chunk), :] * rstd * g_ref[...]
        dst_ref[pl.ds(dst_row0 + r0, chunk), pl.ds(dst_col0, width)] = y.astype(dst_ref.dtype)
        return carry

    lax.fori_loop(0, rows // chunk, scale, 0)


def _norm_matmul_kernel(x_ref, g_ref, w_ref, cs_ref, o_ref, a_ref, stat_ref):
    @pl.when(pl.program_id(1) == 0)
    def _():
        _norm_into(a_ref, 0, 0, x_ref, g_ref, x_ref.shape[0], stat_ref)

    y = jnp.dot(a_ref[...], w_ref[...], preferred_element_type=F32)
    o_ref[...] = (y * cs_ref[...]).astype(o_ref.dtype)


def _norm_matmul(x, g, w, col0, n, col_scale, out_dtype, *, tm, tn, name):
    m, d = x.shape
    j0 = col0 // tn
    return pl.pallas_call(
        _norm_matmul_kernel,
        grid=(m // tm, n // tn),
        in_specs=[
            pl.BlockSpec((tm, d), lambda i, j: (i, 0)),
            pl.BlockSpec((1, d), lambda i, j: (0, 0)),
            pl.BlockSpec((d, tn), lambda i, j: (0, j0 + j)),
            pl.BlockSpec((1, tn), lambda i, j: (0, j)),
        ],
        out_specs=pl.BlockSpec((tm, tn), lambda i, j: (i, j)),
        out_shape=jax.ShapeDtypeStruct((m, n), out_dtype),
        scratch_shapes=[pltpu.VMEM((tm, d), BF16), pltpu.VMEM((tm, V7X_LANES), F32)],
        compiler_params=_params(("parallel", "arbitrary")),
        name=name,
    )(x, g, w, col_scale)


MASKED_SCORE = -1e30


def _attn_scores(q, k, diagonal):
    tq, tk = q.shape[0], k.shape[0]
    z = lax.dot_general(q, k, (((1,), (1,)), ((), ())), preferred_element_type=F32)
    sp = jnp.maximum(z, 0.0) + jnp.log(1.0 + jnp.exp(-jnp.abs(z)))
    if diagonal:
        row = lax.broadcasted_iota(jnp.int32, (tq, tk), 0)
        col = lax.broadcasted_iota(jnp.int32, (tq, tk), 1)
        mask = col < row
        sp = jnp.where(mask, sp, 0.0)
        z = jnp.where(mask, z, MASKED_SCORE)
    return z, sp.astype(BF16)


def _attn_accumulate(z, sp, v, negu_ref, acc_ref, csum_ref, cols):
    tk = z.shape[1]
    s = jnp.dot(sp, negu_ref[...], preferred_element_type=F32)
    csum = csum_ref[:, cols]
    a = jnp.exp(z + s + jnp.concatenate([csum] * (tk // V7X_LANES), axis=1))
    acc_ref[:, cols] += jnp.dot(a.astype(BF16), v, preferred_element_type=F32)
    csum_ref[:, cols] = csum + jnp.broadcast_to(s[:, :1], csum.shape)


def _attn_kernel(q_ref, k_ref, v_ref, negu_ref, o_ref, acc_ref, csum_ref, z0_ref, sp0_ref,
                 z1_ref, sp1_ref, *, tq, tk, heads):
    qi = pl.program_id(2)
    acc_ref[...] = jnp.zeros_like(acc_ref)
    csum_ref[...] = jnp.zeros_like(csum_ref)

    def key_rows(n):
        return pl.ds(pl.multiple_of((qi - n) * tk, tk), tk)

    def scores(n, z_ref, sp_ref, diagonal=False):
        rows = key_rows(n)
        for g in range(heads):
            cols = slice(g * HEAD_DIM, (g + 1) * HEAD_DIM)
            z, sp = _attn_scores(q_ref[:, cols], k_ref[rows, cols], diagonal)
            z_ref[:, g * tk:(g + 1) * tk] = z
            sp_ref[:, g * tk:(g + 1) * tk] = sp

    def accumulate(n, z_ref, sp_ref):
        rows = key_rows(n)
        for g in range(heads):
            cols = slice(g * HEAD_DIM, (g + 1) * HEAD_DIM)
            tile = slice(g * tk, (g + 1) * tk)
            _attn_accumulate(z_ref[:, tile], sp_ref[:, tile], v_ref[rows, cols], negu_ref,
                             acc_ref, csum_ref, cols)

    scores(0, z0_ref, sp0_ref, diagonal=True)

    def pair(p, carry):
        n = 2 * p
        accumulate(n, z0_ref, sp0_ref)
        scores(n + 1, z1_ref, sp1_ref)
        accumulate(n + 1, z1_ref, sp1_ref)
        scores(n + 2, z0_ref, sp0_ref)
        return carry

    lax.fori_loop(0, qi // 2, pair, 0)
    n_done = 2 * (qi // 2)

    @pl.when(qi % 2 == 1)
    def _():
        accumulate(n_done, z0_ref, sp0_ref)
        scores(n_done + 1, z1_ref, sp1_ref)
        accumulate(n_done + 1, z1_ref, sp1_ref)

    @pl.when(qi % 2 == 0)
    def _():
        accumulate(n_done, z0_ref, sp0_ref)

    o_ref[...] = acc_ref[...].astype(o_ref.dtype)


def _attention(qkv, batch, seq, n_heads, *, tq, heads):
    tk = tq
    nq = seq // tq
    width = heads * HEAD_DIM
    groups = n_heads // heads
    row = lax.broadcasted_iota(jnp.int32, (tk, tk), 0)
    col = lax.broadcasted_iota(jnp.int32, (tk, tk), 1)
    negu = -(row >= col).astype(BF16)
    kernel = functools.partial(_attn_kernel, tq=tq, tk=tk, heads=heads)
    return pl.pallas_call(
        kernel,
        grid=(batch, groups, nq),
        in_specs=[
            pl.BlockSpec((tq, width), lambda b, h, i: (b * nq + i, h)),
            pl.BlockSpec((seq, width), lambda b, h, i: (b, groups + h)),
            pl.BlockSpec((seq, width), lambda b, h, i: (b, 2 * groups + h)),
            pl.BlockSpec((tk, tk), lambda b, h, i: (0, 0)),
        ],
        out_specs=pl.BlockSpec((tq, width), lambda b, h, i: (b * nq + i, h)),
        out_shape=jax.ShapeDtypeStruct((batch * seq, n_heads * HEAD_DIM), F32),
        scratch_shapes=[
            pltpu.VMEM((tq, width), F32),
            pltpu.VMEM((tq, width), F32),
            pltpu.VMEM((tq, heads * tk), F32),
            pltpu.VMEM((tq, heads * tk), BF16),
            pltpu.VMEM((tq, heads * tk), F32),
            pltpu.VMEM((tq, heads * tk), BF16),
        ],
        compiler_params=_params(("parallel", "parallel", "arbitrary")),
        name="stickbreak_attention",
    )(qkv, qkv, qkv, negu)


def _softplus(x):
    return jnp.maximum(x, 0.0) + jnp.log1p(jnp.exp(-jnp.abs(x)))


def _sigmoid(x):
    return 0.5 * (jnp.tanh(0.5 * x) + 1.0)


def _rglru_kernel(xr_ref, yr_ref, wc_ref, bc_ref, wa_ref, ba_ref, wx_ref, bx_ref, lam_ref,
                  o_ref, xext_ref, a_ref, b_ref, h_ref, *, ts, tw):
    si = pl.program_id(2)

    @pl.when(si == 0)
    def _():
        xext_ref[pl.ds(0, HALO), :] = jnp.zeros((HALO, tw), F32)
        h_ref[...] = jnp.zeros_like(h_ref)

    @pl.when(si != 0)
    def _():
        xext_ref[pl.ds(0, HALO), :] = xext_ref[pl.ds(ts, HALO), :]

    xext_ref[pl.ds(HALO, ts), :] = xr_ref[...]

    xc = bc_ref[...] + wc_ref[pl.ds(REC_CONV - 1, 1), :] * xr_ref[...]
    for k in range(REC_CONV - 1):
        shift = REC_CONV - 1 - k
        xc = xc + wc_ref[pl.ds(k, 1), :] * xext_ref[pl.ds(HALO - shift, ts), :]

    neg_c_sp = -RG_C * _softplus(-lam_ref[...])
    xc16 = xc.astype(BF16)
    for n in range(tw // V7X_LANES):
        cols = slice(n * V7X_LANES, (n + 1) * V7X_LANES)
        xb = xc16[:, cols]
        r = _sigmoid(jnp.dot(xb, wa_ref[n], preferred_element_type=F32) + ba_ref[:, cols])
        i = _sigmoid(jnp.dot(xb, wx_ref[n], preferred_element_type=F32) + bx_ref[:, cols])
        log_a = neg_c_sp[:, cols] * r
        a = jnp.exp(log_a)
        mult = jnp.sqrt((1.0 - a) * (1.0 + a))
        a_ref[:, cols] = a
        b_ref[:, cols] = mult * (i * xc[:, cols])

    row = lax.broadcasted_iota(jnp.int32, (V7X_SUBLANES, tw), 0)

    def group(gi, h_prev):
        r0 = pl.multiple_of(gi * V7X_SUBLANES, V7X_SUBLANES)
        a = a_ref[pl.ds(r0, V7X_SUBLANES), :]
        b = b_ref[pl.ds(r0, V7X_SUBLANES), :]
        for d in (1, 2, 4):
            keep = row >= d
            a_sh = jnp.where(keep, pltpu.roll(a, d, 0), 1.0)
            b_sh = jnp.where(keep, pltpu.roll(b, d, 0), 0.0)
            b = a * b_sh + b
            a = a * a_sh
        h = a * h_prev + b
        b_ref[pl.ds(r0, V7X_SUBLANES), :] = h
        return jnp.broadcast_to(h[V7X_SUBLANES - 1:, :], (V7X_SUBLANES, tw))

    h_last = lax.fori_loop(0, ts // V7X_SUBLANES, group, h_ref[...])
    h_ref[...] = h_last
    o_ref[...] = jax.nn.gelu(yr_ref[...]) * b_ref[...]


def _rglru(xy, w_rconv, b_rconv, w_rg_a, b_rg_a, w_rg_x, b_rg_x, lam, batch, seq, *, ts, tw):
    width = lam.shape[-1]
    nw = width // tw
    ns = seq // ts
    gb = tw // V7X_LANES
    row_spec = pl.BlockSpec((1, tw), lambda b, w, s: (0, w))
    kernel = functools.partial(_rglru_kernel, ts=ts, tw=tw)
    return pl.pallas_call(
        kernel,
        grid=(batch, nw, ns),
        in_specs=[
            pl.BlockSpec((ts, tw), lambda b, w, s: (b * ns + s, w)),
            pl.BlockSpec((ts, tw), lambda b, w, s: (b * ns + s, nw + w)),
            pl.BlockSpec((REC_CONV, tw), lambda b, w, s: (0, w)),
            row_spec,
            pl.BlockSpec((gb, V7X_LANES, V7X_LANES), lambda b, w, s: (w, 0, 0)),
            row_spec,
            pl.BlockSpec((gb, V7X_LANES, V7X_LANES), lambda b, w, s: (w, 0, 0)),
            row_spec,
            row_spec,
        ],
        out_specs=pl.BlockSpec((ts, tw), lambda b, w, s: (b * ns + s, w)),
        out_shape=jax.ShapeDtypeStruct((batch * seq, width), F32),
        scratch_shapes=[
            pltpu.VMEM((ts + HALO, tw), F32),
            pltpu.VMEM((ts, tw), F32),
            pltpu.VMEM((ts, tw), F32),
            pltpu.VMEM((V7X_SUBLANES, tw), F32),
        ],
        compiler_params=_params(("parallel", "parallel", "arbitrary")),
        name="rglru_branch",
    )(xy, xy, w_rconv, b_rconv.reshape(1, width), w_rg_a.astype(BF16), b_rg_a.reshape(1, width),
      w_rg_x.astype(BF16), b_rg_x.reshape(1, width), lam.reshape(1, width))


def _outproj_kernel(att_ref, rec_ref, ga_ref, gr_ref, w_ref, res_ref, o_ref, a_ref, stat_ref):
    @pl.when(pl.program_id(1) == 0)
    def _():
        _norm_into(a_ref, 0, 0, att_ref, ga_ref, att_ref.shape[0], stat_ref)
        _norm_into(a_ref, 0, att_ref.shape[1], rec_ref, gr_ref, rec_ref.shape[0], stat_ref)

    o_ref[...] = res_ref[...] + jnp.dot(a_ref[...], w_ref[...], preferred_element_type=F32)


def _outproj(att, rec, g_att, g_rec, w, res, *, tm, tn):
    m, wa = att.shape
    wr = rec.shape[1]
    n = w.shape[1]
    return pl.pallas_call(
        _outproj_kernel,
        grid=(m // tm, n // tn),
        in_specs=[
            pl.BlockSpec((tm, wa), lambda i, j: (i, 0)),
            pl.BlockSpec((tm, wr), lambda i, j: (i, 0)),
            pl.BlockSpec((1, wa), lambda i, j: (0, 0)),
            pl.BlockSpec((1, wr), lambda i, j: (0, 0)),
            pl.BlockSpec((wa + wr, tn), lambda i, j: (0, j)),
            pl.BlockSpec((tm, tn), lambda i, j: (i, j)),
        ],
        out_specs=pl.BlockSpec((tm, tn), lambda i, j: (i, j)),
        out_shape=jax.ShapeDtypeStruct((m, n), F32),
        scratch_shapes=[pltpu.VMEM((tm, wa + wr), BF16), pltpu.VMEM((tm, V7X_LANES), F32)],
        compiler_params=_params(("parallel", "arbitrary")),
        name="out_projection",
    )(att, rec, g_att, g_rec, w, res)


def _ffn_up_kernel(h_ref, halo_ref, g_ref, wg_ref, wu_ref, cg_ref, cu_ref, bg_ref, bu_ref,
                   o_ref, a_ref, stat_ref, *, tm, seq):
    i = pl.program_id(0)

    @pl.when(pl.program_id(1) == 0)
    def _():
        _norm_into(a_ref, FFN_HALO, 0, h_ref, g_ref, tm, stat_ref)
        seq_start = (i * tm) % seq == 0
        halo = _rms_rows(halo_ref[...], g_ref[...])
        a_ref[pl.ds(0, FFN_HALO), :] = jnp.where(seq_start, 0.0, halo).astype(a_ref.dtype)

    a = a_ref[...]

    def conv(w_ref, c_ref, b_ref):
        y = jnp.dot(a, w_ref[...], preferred_element_type=F32)
        out = b_ref[...] + c_ref[pl.ds(FF_CONV - 1, 1), :] * y[FFN_HALO:, :]
        for k in range(FF_CONV - 1):
            shift = FF_CONV - 1 - k
            out = out + c_ref[pl.ds(k, 1), :] * y[FFN_HALO - shift:FFN_HALO - shift + tm, :]
        return out

    gate = conv(wg_ref, cg_ref, bg_ref)
    up = conv(wu_ref, cu_ref, bu_ref)
    o_ref[...] = (jax.nn.gelu(gate) * up).astype(o_ref.dtype)


def _ffn_up(h, g, w_up, w_conv, b_conv, seq, *, tm, tf):
    m, d = h.shape
    f = w_up.shape[1] // 2
    nf = f // tf
    kernel = functools.partial(_ffn_up_kernel, tm=tm, seq=seq)
    halo_blocks = tm // FFN_HALO
    b_conv = b_conv.reshape(1, 2 * f)
    return pl.pallas_call(
        kernel,
        grid=(m // tm, nf),
        in_specs=[
            pl.BlockSpec((tm, d), lambda i, j: (i, 0), pipeline_mode=pl.Buffered(1)),
            pl.BlockSpec((FFN_HALO, d), lambda i, j: (jnp.maximum(i * halo_blocks - 1, 0), 0)),
            pl.BlockSpec((1, d), lambda i, j: (0, 0)),
            pl.BlockSpec((d, tf), lambda i, j: (0, j)),
            pl.BlockSpec((d, tf), lambda i, j: (0, nf + j)),
            pl.BlockSpec((FF_CONV, tf), lambda i, j: (0, j)),
            pl.BlockSpec((FF_CONV, tf), lambda i, j: (0, nf + j)),
            pl.BlockSpec((1, tf), lambda i, j: (0, j)),
            pl.BlockSpec((1, tf), lambda i, j: (0, nf + j)),
        ],
        out_specs=pl.BlockSpec((tm, tf), lambda i, j: (i, j)),
        out_shape=jax.ShapeDtypeStruct((m, f), BF16),
        scratch_shapes=[pltpu.VMEM((FFN_HALO + tm, d), BF16), pltpu.VMEM((tm, V7X_LANES), F32)],
        compiler_params=_params(("parallel", "arbitrary")),
        name="ffn_up_conv_gate",
    )(h, h, g, w_up, w_up, w_conv, w_conv, b_conv, b_conv)


def _matmul_res_kernel(a_ref, w_ref, res_ref, o_ref, acc_ref):
    k = pl.program_id(2)

    @pl.when((pl.program_id(0) == 0) & (pl.program_id(1) == 0) & (k == 0))
    def _():
        acc_ref[...] = jnp.zeros_like(acc_ref)

    base = jnp.where(k == 0, res_ref[...], acc_ref[...])
    total = base + jnp.dot(a_ref[...], w_ref[...], preferred_element_type=F32)
    acc_ref[...] = total
    o_ref[...] = total


def _matmul_res(a, w, res, *, tm, tn, tk):
    m, kdim = a.shape
    n = w.shape[1]
    return pl.pallas_call(
        _matmul_res_kernel,
        grid=(m // tm, n // tn, kdim // tk),
        in_specs=[
            pl.BlockSpec((tm, tk), lambda i, j, k: (i, k)),
            pl.BlockSpec((tk, tn), lambda i, j, k: (k, j)),
            pl.BlockSpec((tm, tn), lambda i, j, k: (i, j)),
        ],
        out_specs=pl.BlockSpec((tm, tn), lambda i, j, k: (i, j)),
        out_shape=jax.ShapeDtypeStruct((m, n), F32),
        scratch_shapes=[pltpu.VMEM((tm, tn), F32)],
        compiler_params=_params(("arbitrary", "arbitrary", "arbitrary")),
        name="ffn_down_projection",
    )(a, w, res)


def _ple_kernel(h_ref, g_ref, wg_ref, p_ref, wp_ref, res_ref, o_ref, a_ref, stat_ref):
    @pl.when(pl.program_id(1) == 0)
    def _():
        _norm_into(a_ref, 0, 0, h_ref, g_ref, h_ref.shape[0], stat_ref)

    gate = jax.nn.sigmoid(jnp.dot(a_ref[...], wg_ref[...], preferred_element_type=F32))
    emb = jnp.dot(p_ref[...].astype(BF16), wp_ref[...], preferred_element_type=F32)
    o_ref[...] = res_ref[...] + emb * gate


def _ple(h, g, w_gate, p, w_ple, *, tm, tn):
    m, d = h.shape
    n = w_gate.shape[1]
    pd = p.shape[1]
    return pl.pallas_call(
        _ple_kernel,
        grid=(m // tm, n // tn),
        in_specs=[
            pl.BlockSpec((tm, d), lambda i, j: (i, 0)),
            pl.BlockSpec((1, d), lambda i, j: (0, 0)),
            pl.BlockSpec((d, tn), lambda i, j: (0, j)),
            pl.BlockSpec((tm, pd), lambda i, j: (i, 0)),
            pl.BlockSpec((pd, tn), lambda i, j: (0, j)),
            pl.BlockSpec((tm, tn), lambda i, j: (i, j)),
        ],
        out_specs=pl.BlockSpec((tm, tn), lambda i, j: (i, j)),
        out_shape=jax.ShapeDtypeStruct((m, n), F32),
        scratch_shapes=[pltpu.VMEM((tm, d), BF16), pltpu.VMEM((tm, V7X_LANES), F32)],
        compiler_params=_params(("parallel", "arbitrary")),
        name="ple_gate",
    )(h, g, w_gate, p, w_ple, h)


def _rmsnorm_kernel(x_ref, g_ref, o_ref):
    o_ref[...] = _rms_rows(x_ref[...], g_ref[...])


def _rmsnorm(x, g, *, tm):
    m, d = x.shape
    return pl.pallas_call(
        _rmsnorm_kernel,
        grid=(m // tm,),
        in_specs=[pl.BlockSpec((tm, d), lambda i: (i, 0)), pl.BlockSpec((1, d), lambda i: (0, 0))],
        out_specs=pl.BlockSpec((tm, d), lambda i: (i, 0)),
        out_shape=jax.ShapeDtypeStruct((m, d), F32),
        compiler_params=_params(("parallel",)),
        name="final_rmsnorm",
    )(x, g)


def kernel(x, p, g_mix, w_in, w_rconv, b_rconv, w_rg_a, b_rg_a, w_rg_x, b_rg_x, lam, g_att_out, g_rec_out, w_out, g_ffn, w_up, w_ffconv, b_ffconv, w_down, g_ple, w_ple, w_ple_gate, g_final):
    batch, seq, d_model = x.shape
    depth = w_in.shape[0]
    lru_width = lam.shape[-1]
    att_width = w_out.shape[1] - lru_width
    n_heads = att_width // HEAD_DIM
    m = batch * seq

    h = x.reshape(m, d_model)
    for l in range(depth):
        w_in_l = w_in[l].astype(BF16)
        qkv_scale = jnp.concatenate(
            [jnp.full((1, att_width), 1.0 / math.sqrt(HEAD_DIM), F32),
             jnp.ones((1, 2 * att_width), F32)], axis=1)
        g_mix_l = g_mix[l].reshape(1, d_model)
        qkv = _norm_matmul(h, g_mix_l, w_in_l, 0, 3 * att_width, qkv_scale, BF16,
                           tm=512, tn=1024, name="in_projection_qkv")
        xy = _norm_matmul(h, g_mix_l, w_in_l, 3 * att_width, 2 * lru_width,
                          jnp.ones((1, 2 * lru_width), F32), F32,
                          tm=512, tn=1024, name="in_projection_lru")
        att = _attention(qkv, batch, seq, n_heads, tq=256, heads=4)
        rec = _rglru(xy, w_rconv[l], b_rconv[l], w_rg_a[l], b_rg_a[l], w_rg_x[l], b_rg_x[l],
                     lam[l], batch, seq, ts=512, tw=512)
        h = _outproj(att, rec, g_att_out[l].reshape(1, att_width),
                     g_rec_out[l].reshape(1, lru_width), w_out[l].astype(BF16), h,
                     tm=512, tn=1024)
        act = _ffn_up(h, g_ffn[l].reshape(1, d_model), w_up[l].astype(BF16), w_ffconv[l],
                      b_ffconv[l], seq, tm=1024, tf=512)
        h = _matmul_res(act, w_down[l].astype(BF16), h, tm=1024, tn=1024, tk=2048)
        h = _ple(h, g_ple[l].reshape(1, d_model), w_ple_gate[l].astype(BF16),
                 p[l].reshape(m, -1), w_ple[l].astype(BF16), tm=512, tn=1024)
    out = _rmsnorm(h, g_final.reshape(1, d_model), tm=256)
    return out.reshape(batch, seq, d_model)
```

```python
import functools
import math

import jax
import jax.numpy as jnp
from jax import lax
from jax.experimental import pallas as pl
from jax.experimental.pallas import tpu as pltpu

F32 = jnp.float32
BF16 = jnp.bfloat16

EPS = 1e-6
HEAD_DIM = 128
N_LRU_BLOCKS = 16
RG_C = 8.0
REC_CONV = 4
FF_CONV = 3

V7X_LANES = 128
V7X_SUBLANES = 8
V7X_VMEM_LIMIT_BYTES = 60000 * 1024

V7X_BF16_ROWS = 2 * V7X_SUBLANES

HALO = V7X_SUBLANES
FFN_HALO = V7X_BF16_ROWS
NORM_ROWS = 32


def _params(semantics, vmem_bytes=V7X_VMEM_LIMIT_BYTES):
    return pltpu.CompilerParams(dimension_semantics=semantics, vmem_limit_bytes=vmem_bytes)


def _rms_rows(x, g):
    ms = jnp.mean(x * x, axis=-1, keepdims=True)
    return x * lax.rsqrt(ms + EPS) * g


def _norm_into(dst_ref, dst_row0, dst_col0, src_ref, g_ref, rows, stat_ref):
    width = src_ref.shape[-1]
    lane_tiles = width // V7X_LANES
    chunk = min(NORM_ROWS, rows)

    def sum_squares(c, carry):
        r0 = pl.multiple_of(c * chunk, chunk)
        x = src_ref[pl.ds(r0, chunk), :]
        sq = x * x
        parts = [sq[:, k * V7X_LANES:(k + 1) * V7X_LANES] for k in range(lane_tiles)]
        while len(parts) > 1:
            parts = [a + b for a, b in zip(parts[0::2], parts[1::2])] + parts[len(parts) & ~1:]
        stat_ref[pl.ds(r0, chunk), :] = parts[0]
        return carry

    lax.fori_loop(0, rows // chunk, sum_squares, 0)
    ms = jnp.sum(stat_ref[pl.ds(0, rows), :], axis=-1, keepdims=True) * (1.0 / width)
    stat_ref[pl.ds(0, rows), :] = jnp.broadcast_to(lax.rsqrt(ms + EPS), (rows, V7X_LANES))

    def scale(c, carry):
        r0 = pl.multiple_of(c * chunk, chunk)
        rstd = jnp.concatenate([stat_ref[pl.ds(r0, chunk), :]] * lane_tiles, axis=1)
        y = src_ref[pl.ds(r0, chunk), :] * rstd * g_ref[...]
        dst_ref[pl.ds(dst_row0 + r0, chunk), pl.ds(dst_col0, width)] = y.astype(dst_ref.dtype)
        return carry

    lax.fori_loop(0, rows // chunk, scale, 0)


def _row_tile_prologue(sources, consume):
    i = pl.program_id(0)

    def copies(tile):
        return [pltpu.make_async_copy(hbm.at[pl.ds(tile * buf.shape[0], buf.shape[0]), :], buf, sem)
                for hbm, buf, sem in sources]

    @pl.when(pl.program_id(1) == 0)
    def _():
        @pl.when(i == 0)
        def _():
            for c in copies(0):
                c.start()

        for c in copies(i):
            c.wait()
        consume()

        @pl.when(i + 1 < pl.num_programs(0))
        def _():
            for c in copies(i + 1):
                c.start()


def _norm_matmul_kernel(x_hbm, g_ref, w_ref, cs_ref, o_ref, x_ref, a_ref, stat_ref, sem):
    _row_tile_prologue(
        [(x_hbm, x_ref, sem)],
        lambda: _norm_into(a_ref, 0, 0, x_ref, g_ref, x_ref.shape[0], stat_ref))
    y = jnp.dot(a_ref[...], w_ref[...], preferred_element_type=F32)
    o_ref[...] = (y * cs_ref[...]).astype(o_ref.dtype)


def _norm_matmul(x, g, w, col0, n, col_scale, out_dtype, *, tm, tn, name):
    m, d = x.shape
    j0 = col0 // tn
    return pl.pallas_call(
        _norm_matmul_kernel,
        grid=(m // tm, n // tn),
        in_specs=[
            pl.BlockSpec(memory_space=pl.ANY),
            pl.BlockSpec((1, d), lambda i, j: (0, 0)),
            pl.BlockSpec((d, tn), lambda i, j: (0, j0 + j)),
            pl.BlockSpec((1, tn), lambda i, j: (0, j)),
        ],
        out_specs=pl.BlockSpec((tm, tn), lambda i, j: (i, j)),
        out_shape=jax.ShapeDtypeStruct((m, n), out_dtype),
        scratch_shapes=[
            pltpu.VMEM((tm, d), F32),
            pltpu.VMEM((tm, d), BF16),
            pltpu.VMEM((tm, V7X_LANES), F32),
            pltpu.SemaphoreType.DMA(()),
        ],
        compiler_params=_params(("arbitrary", "arbitrary")),
        name=name,
    )(x, g, w, col_scale)


MASKED_SCORE = -1e30


def _attn_scores(q, k, diagonal):
    tq, tk = q.shape[0], k.shape[0]
    z = lax.dot_general(q, k, (((1,), (1,)), ((), ())), preferred_element_type=F32)
    sp = jnp.maximum(z, 0.0) + jnp.log(1.0 + jnp.exp(-jnp.abs(z)))
    if diagonal:
        row = lax.broadcasted_iota(jnp.int32, (tq, tk), 0)
        col = lax.broadcasted_iota(jnp.int32, (tq, tk), 1)
        mask = col < row
        sp = jnp.where(mask, sp, 0.0)
        z = jnp.where(mask, z, MASKED_SCORE)
    return z, sp.astype(BF16)


def _attn_accumulate(z, sp, v, negu_ref, acc_ref, csum_ref, cols):
    tk = z.shape[1]
    s = jnp.dot(sp, negu_ref[...], preferred_element_type=F32)
    csum = csum_ref[:, cols]
    a = jnp.exp(z + s + jnp.concatenate([csum] * (tk // V7X_LANES), axis=1))
    acc_ref[:, cols] += jnp.dot(a.astype(BF16), v, preferred_element_type=F32)
    csum_ref[:, cols] = csum + jnp.broadcast_to(s[:, :1], csum.shape)


def _attn_kernel(q_ref, k_ref, v_ref, negu_ref, o_ref, acc_ref, csum_ref, z0_ref, sp0_ref,
                 z1_ref, sp1_ref, *, tq, tk, heads):
    qi = pl.program_id(2)
    acc_ref[...] = jnp.zeros_like(acc_ref)
    csum_ref[...] = jnp.zeros_like(csum_ref)

    def key_rows(n):
        return pl.ds(pl.multiple_of((qi - n) * tk, tk), tk)

    def scores(n, z_ref, sp_ref, diagonal=False):
        rows = key_rows(n)
        for g in range(heads):
            cols = slice(g * HEAD_DIM, (g + 1) * HEAD_DIM)
            z, sp = _attn_scores(q_ref[:, cols], k_ref[rows, cols], diagonal)
            z_ref[:, g * tk:(g + 1) * tk] = z
            sp_ref[:, g * tk:(g + 1) * tk] = sp

    def accumulate(n, z_ref, sp_ref):
        rows = key_rows(n)
        for g in range(heads):
            cols = slice(g * HEAD_DIM, (g + 1) * HEAD_DIM)
            tile = slice(g * tk, (g + 1) * tk)
            _attn_accumulate(z_ref[:, tile], sp_ref[:, tile], v_ref[rows, cols], negu_ref,
                             acc_ref, csum_ref, cols)

    scores(0, z0_ref, sp0_ref, diagonal=True)

    def pair(p, carry):
        n = 2 * p
        accumulate(n, z0_ref, sp0_ref)
        scores(n + 1, z1_ref, sp1_ref)
        accumulate(n + 1, z1_ref, sp1_ref)
        scores(n + 2, z0_ref, sp0_ref)
        return carry

    lax.fori_loop(0, qi // 2, pair, 0)
    n_done = 2 * (qi // 2)

    @pl.when(qi % 2 == 1)
    def _():
        accumulate(n_done, z0_ref, sp0_ref)
        scores(n_done + 1, z1_ref, sp1_ref)
        accumulate(n_done + 1, z1_ref, sp1_ref)

    @pl.when(qi % 2 == 0)
    def _():
        accumulate(n_done, z0_ref, sp0_ref)

    o_ref[...] = acc_ref[...].astype(o_ref.dtype)


def _attention(qkv, batch, seq, n_heads, *, tq, heads):
    tk = tq
    nq = seq // tq
    width = heads * HEAD_DIM
    groups = n_heads // heads
    row = lax.broadcasted_iota(jnp.int32, (tk, tk), 0)
    col = lax.broadcasted_iota(jnp.int32, (tk, tk), 1)
    negu = -(row >= col).astype(BF16)
    kernel = functools.partial(_attn_kernel, tq=tq, tk=tk, heads=heads)
    return pl.pallas_call(
        kernel,
        grid=(batch, groups, nq),
        in_specs=[
            pl.BlockSpec((tq, width), lambda b, h, i: (b * nq + i, h)),
            pl.BlockSpec((seq, width), lambda b, h, i: (b, groups + h)),
            pl.BlockSpec((seq, width), lambda b, h, i: (b, 2 * groups + h)),
            pl.BlockSpec((tk, tk), lambda b, h, i: (0, 0)),
        ],
        out_specs=pl.BlockSpec((tq, width), lambda b, h, i: (b * nq + i, h)),
        out_shape=jax.ShapeDtypeStruct((batch * seq, n_heads * HEAD_DIM), F32),
        scratch_shapes=[
            pltpu.VMEM((tq, width), F32),
            pltpu.VMEM((tq, width), F32),
            pltpu.VMEM((tq, heads * tk), F32),
            pltpu.VMEM((tq, heads * tk), BF16),
            pltpu.VMEM((tq, heads * tk), F32),
            pltpu.VMEM((tq, heads * tk), BF16),
        ],
        compiler_params=_params(("parallel", "parallel", "arbitrary")),
        name="stickbreak_attention",
    )(qkv, qkv, qkv, negu)


def _softplus(x):
    return jnp.maximum(x, 0.0) + jnp.log1p(jnp.exp(-jnp.abs(x)))


def _sigmoid(x):
    return 0.5 * (jnp.tanh(0.5 * x) + 1.0)


def _rglru_kernel(xr_ref, yr_ref, wc_ref, bc_ref, wa_ref, ba_ref, wx_ref, bx_ref, lam_ref,
                  o_ref, xext_ref, a_ref, b_ref, h_ref, *, ts, tw):
    si = pl.program_id(2)

    @pl.when(si == 0)
    def _():
        xext_ref[pl.ds(0, HALO), :] = jnp.zeros((HALO, tw), F32)
        h_ref[...] = jnp.zeros_like(h_ref)

    @pl.when(si != 0)
    def _():
        xext_ref[pl.ds(0, HALO), :] = xext_ref[pl.ds(ts, HALO), :]

    xext_ref[pl.ds(HALO, ts), :] = xr_ref[...]

    xc = bc_ref[...] + wc_ref[pl.ds(REC_CONV - 1, 1), :] * xr_ref[...]
    for k in range(REC_CONV - 1):
        shift = REC_CONV - 1 - k
        xc = xc + wc_ref[pl.ds(k, 1), :] * xext_ref[pl.ds(HALO - shift, ts), :]

    neg_c_sp = -RG_C * _softplus(-lam_ref[...])
    xc16 = xc.astype(BF16)
    for n in range(tw // V7X_LANES):
        cols = slice(n * V7X_LANES, (n + 1) * V7X_LANES)
        xb = xc16[:, cols]
        r = _sigmoid(jnp.dot(xb, wa_ref[n], preferred_element_type=F32) + ba_ref[:, cols])
        i = _sigmoid(jnp.dot(xb, wx_ref[n], preferred_element_type=F32) + bx_ref[:, cols])
        log_a = neg_c_sp[:, cols] * r
        a = jnp.exp(log_a)
        mult = jnp.sqrt((1.0 - a) * (1.0 + a))
        a_ref[:, cols] = a
        b_ref[:, cols] = mult * (i * xc[:, cols])

    row = lax.broadcasted_iota(jnp.int32, (V7X_SUBLANES, tw), 0)

    def group(gi, h_prev):
        r0 = pl.multiple_of(gi * V7X_SUBLANES, V7X_SUBLANES)
        a = a_ref[pl.ds(r0, V7X_SUBLANES), :]
        b = b_ref[pl.ds(r0, V7X_SUBLANES), :]
        for d in (1, 2, 4):
            keep = row >= d
            a_sh = jnp.where(keep, pltpu.roll(a, d, 0), 1.0)
            b_sh = jnp.where(keep, pltpu.roll(b, d, 0), 0.0)
            b = a * b_sh + b
            a = a * a_sh
        h = a * h_prev + b
        b_ref[pl.ds(r0, V7X_SUBLANES), :] = h
        return jnp.broadcast_to(h[V7X_SUBLANES - 1:, :], (V7X_SUBLANES, tw))

    h_last = lax.fori_loop(0, ts // V7X_SUBLANES, group, h_ref[...])
    h_ref[...] = h_last
    o_ref[...] = jax.nn.gelu(yr_ref[...]) * b_ref[...]


def _rglru(xy, w_rconv, b_rconv, w_rg_a, b_rg_a, w_rg_x, b_rg_x, lam, batch, seq, *, ts, tw):
    width = lam.shape[-1]
    nw = width // tw
    ns = seq // ts
    gb = tw // V7X_LANES
    row_spec = pl.BlockSpec((1, tw), lambda b, w, s: (0, w))
    kernel = functools.partial(_rglru_kernel, ts=ts, tw=tw)
    return pl.pallas_call(
        kernel,
        grid=(batch, nw, ns),
        in_specs=[
            pl.BlockSpec((ts, tw), lambda b, w, s: (b * ns + s, w)),
            pl.BlockSpec((ts, tw), lambda b, w, s: (b * ns + s, nw + w)),
            pl.BlockSpec((REC_CONV, tw), lambda b, w, s: (0, w)),
            row_spec,
            pl.BlockSpec((gb, V7X_LANES, V7X_LANES), lambda b, w, s: (w, 0, 0)),
            row_spec,
            pl.BlockSpec((gb, V7X_LANES, V7X_LANES), lambda b, w, s: (w, 0, 0)),
            row_spec,
            row_spec,
        ],
        out_specs=pl.BlockSpec((ts, tw), lambda b, w, s: (b * ns + s, w)),
        out_shape=jax.ShapeDtypeStruct((batch * seq, width), F32),
        scratch_shapes=[
            pltpu.VMEM((ts + HALO, tw), F32),
            pltpu.VMEM((ts, tw), F32),
            pltpu.VMEM((ts, tw), F32),
            pltpu.VMEM((V7X_SUBLANES, tw), F32),
        ],
        compiler_params=_params(("parallel", "parallel", "arbitrary")),
        name="rglru_branch",
    )(xy, xy, w_rconv, b_rconv.reshape(1, width), w_rg_a.astype(BF16), b_rg_a.reshape(1, width),
      w_rg_x.astype(BF16), b_rg_x.reshape(1, width), lam.reshape(1, width))


def _outproj_kernel(att_hbm, rec_hbm, ga_ref, gr_ref, w_ref, res_ref, o_ref,
                    att_ref, rec_ref, a_ref, stat_ref, att_sem, rec_sem):
    def norms():
        _norm_into(a_ref, 0, 0, att_ref, ga_ref, att_ref.shape[0], stat_ref)
        _norm_into(a_ref, 0, att_ref.shape[1], rec_ref, gr_ref, rec_ref.shape[0], stat_ref)

    _row_tile_prologue([(att_hbm, att_ref, att_sem), (rec_hbm, rec_ref, rec_sem)], norms)
    o_ref[...] = res_ref[...] + jnp.dot(a_ref[...], w_ref[...], preferred_element_type=F32)


def _outproj(att, rec, g_att, g_rec, w, res, *, tm, tn):
    m, wa = att.shape
    wr = rec.shape[1]
    n = w.shape[1]
    return pl.pallas_call(
        _outproj_kernel,
        grid=(m // tm, n // tn),
        in_specs=[
            pl.BlockSpec(memory_space=pl.ANY),
            pl.BlockSpec(memory_space=pl.ANY),
            pl.BlockSpec((1, wa), lambda i, j: (0, 0)),
            pl.BlockSpec((1, wr), lambda i, j: (0, 0)),
            pl.BlockSpec((wa + wr, tn), lambda i, j: (0, j)),
            pl.BlockSpec((tm, tn), lambda i, j: (i, j)),
        ],
        out_specs=pl.BlockSpec((tm, tn), lambda i, j: (i, j)),
        out_shape=jax.ShapeDtypeStruct((m, n), F32),
        scratch_shapes=[
            pltpu.VMEM((tm, wa), F32),
            pltpu.VMEM((tm, wr), F32),
            pltpu.VMEM((tm, wa + wr), BF16),
            pltpu.VMEM((tm, V7X_LANES), F32),
            pltpu.SemaphoreType.DMA(()),
            pltpu.SemaphoreType.DMA(()),
        ],
        compiler_params=_params(("arbitrary", "arbitrary")),
        name="out_projection",
    )(att, rec, g_att, g_rec, w, res)


def _ffn_up_kernel(h_hbm, halo_ref, g_ref, wg_ref, wu_ref, cg_ref, cu_ref, bg_ref, bu_ref,
                   o_ref, h_ref, a_ref, stat_ref, sem, *, tm, seq):
    def norms():
        _norm_into(a_ref, FFN_HALO, 0, h_ref, g_ref, tm, stat_ref)
        seq_start = (pl.program_id(0) * tm) % seq == 0
        halo = _rms_rows(halo_ref[...], g_ref[...])
        a_ref[pl.ds(0, FFN_HALO), :] = jnp.where(seq_start, 0.0, halo).astype(a_ref.dtype)

    _row_tile_prologue([(h_hbm, h_ref, sem)], norms)
    a = a_ref[...]

    def conv(w_ref, c_ref, b_ref):
        y = jnp.dot(a, w_ref[...], preferred_element_type=F32)
        out = b_ref[...] + c_ref[pl.ds(FF_CONV - 1, 1), :] * y[FFN_HALO:, :]
        for k in range(FF_CONV - 1):
            shift = FF_CONV - 1 - k
            out = out + c_ref[pl.ds(k, 1), :] * y[FFN_HALO - shift:FFN_HALO - shift + tm, :]
        return out

    gate = conv(wg_ref, cg_ref, bg_ref)
    up = conv(wu_ref, cu_ref, bu_ref)
    o_ref[...] = (jax.nn.gelu(gate) * up).astype(o_ref.dtype)


def _ffn_up(h, g, w_up, w_conv, b_conv, seq, *, tm, tf):
    m, d = h.shape
    f = w_up.shape[1] // 2
    nf = f // tf
    kernel = functools.partial(_ffn_up_kernel, tm=tm, seq=seq)
    halo_blocks = tm // FFN_HALO
    b_conv = b_conv.reshape(1, 2 * f)
    return pl.pallas_call(
        kernel,
        grid=(m // tm, nf),
        in_specs=[
            pl.BlockSpec(memory_space=pl.ANY),
            pl.BlockSpec((FFN_HALO, d), lambda i, j: (jnp.maximum(i * halo_blocks - 1, 0), 0)),
            pl.BlockSpec((1, d), lambda i, j: (0, 0)),
            pl.BlockSpec((d, tf), lambda i, j: (0, j)),
            pl.BlockSpec((d, tf), lambda i, j: (0, nf + j)),
            pl.BlockSpec((FF_CONV, tf), lambda i, j: (0, j)),
            pl.BlockSpec((FF_CONV, tf), lambda i, j: (0, nf + j)),
            pl.BlockSpec((1, tf), lambda i, j: (0, j)),
            pl.BlockSpec((1, tf), lambda i, j: (0, nf + j)),
        ],
        out_specs=pl.BlockSpec((tm, tf), lambda i, j: (i, j)),
        out_shape=jax.ShapeDtypeStruct((m, f), BF16),
        scratch_shapes=[
            pltpu.VMEM((tm, d), F32),
            pltpu.VMEM((FFN_HALO + tm, d), BF16),
            pltpu.VMEM((tm, V7X_LANES), F32),
            pltpu.SemaphoreType.DMA(()),
        ],
        compiler_params=_params(("arbitrary", "arbitrary")),
        name="ffn_up_conv_gate",
    )(h, h, g, w_up, w_up, w_conv, w_conv, b_conv, b_conv)


def _matmul_res_kernel(a_ref, w_ref, res_ref, o_ref):
    y = jnp.dot(a_ref[...], w_ref[...], preferred_element_type=F32)

    @pl.when(pl.program_id(2) == 0)
    def _():
        o_ref[...] = res_ref[...] + y

    @pl.when(pl.program_id(2) != 0)
    def _():
        o_ref[...] += y


def _matmul_res(a, w, res, *, tm, tn, tk):
    m, kdim = a.shape
    n = w.shape[1]
    return pl.pallas_call(
        _matmul_res_kernel,
        grid=(m // tm, n // tn, kdim // tk),
        in_specs=[
            pl.BlockSpec((tm, tk), lambda i, j, k: (i, k)),
            pl.BlockSpec((tk, tn), lambda i, j, k: (k, j)),
            pl.BlockSpec((tm, tn), lambda i, j, k: (i, j)),
        ],
        out_specs=pl.BlockSpec((tm, tn), lambda i, j, k: (i, j)),
        out_shape=jax.ShapeDtypeStruct((m, n), F32),
        compiler_params=_params(("parallel", "parallel", "arbitrary")),
        name="ffn_down_projection",
    )(a, w, res)


def _ple_kernel(h_hbm, g_ref, wg_ref, p_ref, wp_ref, res_ref, o_ref, h_ref, a_ref, stat_ref, sem):
    _row_tile_prologue(
        [(h_hbm, h_ref, sem)],
        lambda: _norm_into(a_ref, 0, 0, h_ref, g_ref, h_ref.shape[0], stat_ref))
    gate = _sigmoid(jnp.dot(a_ref[...], wg_ref[...], preferred_element_type=F32))
    emb = jnp.dot(p_ref[...].astype(BF16), wp_ref[...], preferred_element_type=F32)
    o_ref[...] = res_ref[...] + emb * gate


def _ple(h, g, w_gate, p, w_ple, *, tm, tn):
    m, d = h.shape
    n = w_gate.shape[1]
    pd = p.shape[1]
    return pl.pallas_call(
        _ple_kernel,
        grid=(m // tm, n // tn),
        in_specs=[
            pl.BlockSpec(memory_space=pl.ANY),
            pl.BlockSpec((1, d), lambda i, j: (0, 0)),
            pl.BlockSpec((d, tn), lambda i, j: (0, j)),
            pl.BlockSpec((tm, pd), lambda i, j: (i, 0)),
            pl.BlockSpec((pd, tn), lambda i, j: (0, j)),
            pl.BlockSpec((tm, tn), lambda i, j: (i, j)),
        ],
        out_specs=pl.BlockSpec((tm, tn), lambda i, j: (i, j)),
        out_shape=jax.ShapeDtypeStruct((m, n), F32),
        scratch_shapes=[
            pltpu.VMEM((tm, d), F32),
            pltpu.VMEM((tm, d), BF16),
            pltpu.VMEM((tm, V7X_LANES), F32),
            pltpu.SemaphoreType.DMA(()),
        ],
        compiler_params=_params(("arbitrary", "arbitrary")),
        name="ple_gate",
    )(h, g, w_gate, p, w_ple, h)


def _rmsnorm_kernel(x_ref, g_ref, o_ref):
    o_ref[...] = _rms_rows(x_ref[...], g_ref[...])


def _rmsnorm(x, g, *, tm):
    m, d = x.shape
    return pl.pallas_call(
        _rmsnorm_kernel,
        grid=(m // tm,),
        in_specs=[pl.BlockSpec((tm, d), lambda i: (i, 0)), pl.BlockSpec((1, d), lambda i: (0, 0))],
        out_specs=pl.BlockSpec((tm, d), lambda i: (i, 0)),
        out_shape=jax.ShapeDtypeStruct((m, d), F32),
        compiler_params=_params(("parallel",)),
        name="final_rmsnorm",
    )(x, g)


def kernel(x, p, g_mix, w_in, w_rconv, b_rconv, w_rg_a, b_rg_a, w_rg_x, b_rg_x, lam, g_att_out, g_rec_out, w_out, g_ffn, w_up, w_ffconv, b_ffconv, w_down, g_ple, w_ple, w_ple_gate, g_final):
    batch, seq, d_model = x.shape
    depth = w_in.shape[0]
    lru_width = lam.shape[-1]
    att_width = w_out.shape[1] - lru_width
    n_heads = att_width // HEAD_DIM
    m = batch * seq

    h = x.reshape(m, d_model)
    for l in range(depth):
        w_in_l = w_in[l].astype(BF16)
        qkv_scale = jnp.concatenate(
            [jnp.full((1, att_width), 1.0 / math.sqrt(HEAD_DIM), F32),
             jnp.ones((1, 2 * att_width), F32)], axis=1)
        g_mix_l = g_mix[l].reshape(1, d_model)
        qkv = _norm_matmul(h, g_mix_l, w_in_l, 0, 3 * att_width, qkv_scale, BF16,
                           tm=1024, tn=1024, name="in_projection_qkv")
        xy = _norm_matmul(h, g_mix_l, w_in_l, 3 * att_width, 2 * lru_width,
                          jnp.ones((1, 2 * lru_width), F32), F32,
                          tm=1024, tn=1024, name="in_projection_lru")
        att = _attention(qkv, batch, seq, n_heads, tq=256, heads=4)
        rec = _rglru(xy, w_rconv[l], b_rconv[l], w_rg_a[l], b_rg_a[l], w_rg_x[l], b_rg_x[l],
                     lam[l], batch, seq, ts=512, tw=512)
        h = _outproj(att, rec, g_att_out[l].reshape(1, att_width),
                     g_rec_out[l].reshape(1, lru_width), w_out[l].astype(BF16), h,
                     tm=1024, tn=512)
        act = _ffn_up(h, g_ffn[l].reshape(1, d_model), w_up[l].astype(BF16), w_ffconv[l],
                      b_ffconv[l], seq, tm=1024, tf=512)
        h = _matmul_res(act, w_down[l].astype(BF16), h, tm=1024, tn=1024, tk=4096)
        h = _ple(h, g_ple[l].reshape(1, d_model), w_ple_gate[l].astype(BF16),
                 p[l].reshape(m, -1), w_ple[l].astype(BF16), tm=1024, tn=512)
    out = _rmsnorm(h, g_final.reshape(1, d_model), tm=256)
    return out.reshape(batch, seq, d_model)
```

```python
import functools
import math

import jax
import jax.numpy as jnp
from jax import lax
from jax.experimental import pallas as pl
from jax.experimental.pallas import tpu as pltpu

F32 = jnp.float32
BF16 = jnp.bfloat16

EPS = 1e-6
HEAD_DIM = 128
N_LRU_BLOCKS = 16
RG_C = 8.0
REC_CONV = 4
FF_CONV = 3

V7X_LANES = 128
V7X_SUBLANES = 8
V7X_VMEM_LIMIT_BYTES = 60000 * 1024

V7X_BF16_ROWS = 2 * V7X_SUBLANES

HALO = V7X_SUBLANES
FFN_HALO = V7X_BF16_ROWS
NORM_ROWS = 32


def _params(semantics, vmem_bytes=V7X_VMEM_LIMIT_BYTES):
    return pltpu.CompilerParams(dimension_semantics=semantics, vmem_limit_bytes=vmem_bytes)


def _rms_rows(x, g):
    ms = jnp.mean(x * x, axis=-1, keepdims=True)
    return x * lax.rsqrt(ms + EPS) * g


def _norm_into(dst_ref, dst_row0, dst_col0, src_ref, g_ref, rows, stat_ref):
    width = src_ref.shape[-1]
    lane_tiles = width // V7X_LANES
    chunk = min(NORM_ROWS, rows)

    def sum_squares(c, carry):
        r0 = pl.multiple_of(c * chunk, chunk)
        x = src_ref[pl.ds(r0, chunk), :]
        sq = x * x
        parts = [sq[:, k * V7X_LANES:(k + 1) * V7X_LANES] for k in range(lane_tiles)]
        while len(parts) > 1:
            parts = [a + b for a, b in zip(parts[0::2], parts[1::2])] + parts[len(parts) & ~1:]
        stat_ref[pl.ds(r0, chunk), :] = parts[0]
        return carry

    lax.fori_loop(0, rows // chunk, sum_squares, 0)
    ms = jnp.sum(stat_ref[pl.ds(0, rows), :], axis=-1, keepdims=True) * (1.0 / width)
    stat_ref[pl.ds(0, rows), :] = jnp.broadcast_to(lax.rsqrt(ms + EPS), (rows, V7X_LANES))

    def scale(c, carry):
        r0 = pl.multiple_of(c * chunk, chunk)
        rstd = jnp.concatenate([stat_ref[pl.ds(r0, chunk), :]] * lane_tiles, axis=1)
        y = src_ref[pl.ds(r0, chunk), :] * rstd * g_ref[...]
        dst_ref[pl.ds(dst_row0 + r0, chunk), pl.ds(dst_col0, width)] = y.astype(dst_ref.dtype)
        return carry

    lax.fori_loop(0, rows // chunk, scale, 0)


def _row_tile_prologue(sources, consume):
    i = pl.program_id(0)

    def copies(tile):
        return [pltpu.make_async_copy(hbm.at[pl.ds(tile * buf.shape[0], buf.shape[0]), :], buf, sem)
                for hbm, buf, sem in sources]

    @pl.when(pl.program_id(1) == 0)
    def _():
        @pl.when(i == 0)
        def _():
            for c in copies(0):
                c.start()

        for c in copies(i):
            c.wait()
        consume()

        @pl.when(i + 1 < pl.num_programs(0))
        def _():
            for c in copies(i + 1):
                c.start()


def _norm_matmul_kernel(x_hbm, g_ref, w_ref, cs_ref, o_ref, x_ref, a_ref, stat_ref, sem):
    _row_tile_prologue(
        [(x_hbm, x_ref, sem)],
        lambda: _norm_into(a_ref, 0, 0, x_ref, g_ref, x_ref.shape[0], stat_ref))
    y = jnp.dot(a_ref[...], w_ref[...], preferred_element_type=F32)
    o_ref[...] = (y * cs_ref[...]).astype(o_ref.dtype)


def _norm_matmul(x, g, w, col0, n, col_scale, out_dtype, *, tm, tn, name):
    m, d = x.shape
    j0 = col0 // tn
    return pl.pallas_call(
        _norm_matmul_kernel,
        grid=(m // tm, n // tn),
        in_specs=[
            pl.BlockSpec(memory_space=pl.ANY),
            pl.BlockSpec((1, d), lambda i, j: (0, 0)),
            pl.BlockSpec((d, tn), lambda i, j: (0, j0 + j)),
            pl.BlockSpec((1, tn), lambda i, j: (0, j)),
        ],
        out_specs=pl.BlockSpec((tm, tn), lambda i, j: (i, j)),
        out_shape=jax.ShapeDtypeStruct((m, n), out_dtype),
        scratch_shapes=[
            pltpu.VMEM((tm, d), F32),
            pltpu.VMEM((tm, d), BF16),
            pltpu.VMEM((tm, V7X_LANES), F32),
            pltpu.SemaphoreType.DMA(()),
        ],
        compiler_params=_params(("arbitrary", "arbitrary")),
        name=name,
    )(x, g, w, col_scale)


MASKED_SCORE = -1e30


def _attn_scores(q, k, diagonal):
    tq, tk = q.shape[0], k.shape[0]
    z = lax.dot_general(q, k, (((1,), (1,)), ((), ())), preferred_element_type=F32)
    sp = jnp.maximum(z, 0.0) + jnp.log(1.0 + jnp.exp(-jnp.abs(z)))
    if diagonal:
        row = lax.broadcasted_iota(jnp.int32, (tq, tk), 0)
        col = lax.broadcasted_iota(jnp.int32, (tq, tk), 1)
        mask = col < row
        sp = jnp.where(mask, sp, 0.0)
        z = jnp.where(mask, z, MASKED_SCORE)
    return z, sp.astype(BF16)


def _attn_accumulate(z, sp, v, negu_ref, acc_ref, csum_ref, cols):
    tk = z.shape[1]
    s = jnp.dot(sp, negu_ref[...], preferred_element_type=F32)
    csum = csum_ref[:, cols]
    a = jnp.exp(z + s + jnp.concatenate([csum] * (tk // V7X_LANES), axis=1))
    acc_ref[:, cols] += jnp.dot(a.astype(BF16), v, preferred_element_type=F32)
    csum_ref[:, cols] = csum + jnp.broadcast_to(s[:, :1], csum.shape)


def _attn_kernel(*refs, tq, tk, heads, n_casts):
    q_ref, k_ref, v_ref, negu_ref = refs[:4]
    cast_in = refs[4:4 + n_casts]
    o_ref = refs[4 + n_casts]
    cast_out = refs[5 + n_casts:5 + 2 * n_casts]
    acc_ref, csum_ref, z0_ref, sp0_ref, z1_ref, sp1_ref = refs[5 + 2 * n_casts:]
    qi = pl.program_id(2)
    acc_ref[...] = jnp.zeros_like(acc_ref)
    csum_ref[...] = jnp.zeros_like(csum_ref)
    for src, dst in zip(cast_in, cast_out):
        dst[...] = src[...].astype(dst.dtype)

    def key_rows(n):
        return pl.ds(pl.multiple_of((qi - n) * tk, tk), tk)

    def scores(n, z_ref, sp_ref, diagonal=False):
        rows = key_rows(n)
        for g in range(heads):
            cols = slice(g * HEAD_DIM, (g + 1) * HEAD_DIM)
            z, sp = _attn_scores(q_ref[:, cols], k_ref[rows, cols], diagonal)
            z_ref[:, g * tk:(g + 1) * tk] = z
            sp_ref[:, g * tk:(g + 1) * tk] = sp

    def accumulate(n, z_ref, sp_ref):
        rows = key_rows(n)
        for g in range(heads):
            cols = slice(g * HEAD_DIM, (g + 1) * HEAD_DIM)
            tile = slice(g * tk, (g + 1) * tk)
            _attn_accumulate(z_ref[:, tile], sp_ref[:, tile], v_ref[rows, cols], negu_ref,
                             acc_ref, csum_ref, cols)

    scores(0, z0_ref, sp0_ref, diagonal=True)

    def pair(p, carry):
        n = 2 * p
        accumulate(n, z0_ref, sp0_ref)
        scores(n + 1, z1_ref, sp1_ref)
        accumulate(n + 1, z1_ref, sp1_ref)
        scores(n + 2, z0_ref, sp0_ref)
        return carry

    lax.fori_loop(0, qi // 2, pair, 0)
    n_done = 2 * (qi // 2)

    @pl.when(qi % 2 == 1)
    def _():
        accumulate(n_done, z0_ref, sp0_ref)
        scores(n_done + 1, z1_ref, sp1_ref)
        accumulate(n_done + 1, z1_ref, sp1_ref)

    @pl.when(qi % 2 == 0)
    def _():
        accumulate(n_done, z0_ref, sp0_ref)

    o_ref[...] = acc_ref[...].astype(o_ref.dtype)


def _attention(qkv, casts, batch, seq, n_heads, *, tq, heads):
    tk = tq
    nq = seq // tq
    width = heads * HEAD_DIM
    groups = n_heads // heads
    steps = batch * groups * nq
    row = lax.broadcasted_iota(jnp.int32, (tk, tk), 0)
    col = lax.broadcasted_iota(jnp.int32, (tk, tk), 1)
    negu = -(row >= col).astype(BF16)
    kernel = functools.partial(_attn_kernel, tq=tq, tk=tk, heads=heads, n_casts=len(casts))
    for w in casts:
        assert w.shape[0] % (steps * V7X_BF16_ROWS) == 0, w.shape
    slab_specs = [pl.BlockSpec((w.shape[0] // steps, w.shape[1]),
                               lambda b, h, i: ((b * groups + h) * nq + i, 0)) for w in casts]
    outs = pl.pallas_call(
        kernel,
        grid=(batch, groups, nq),
        in_specs=[
            pl.BlockSpec((tq, width), lambda b, h, i: (b * nq + i, h)),
            pl.BlockSpec((seq, width), lambda b, h, i: (b, groups + h)),
            pl.BlockSpec((seq, width), lambda b, h, i: (b, 2 * groups + h)),
            pl.BlockSpec((tk, tk), lambda b, h, i: (0, 0)),
        ] + slab_specs,
        out_specs=[pl.BlockSpec((tq, width), lambda b, h, i: (b * nq + i, h))] + slab_specs,
        out_shape=[jax.ShapeDtypeStruct((batch * seq, n_heads * HEAD_DIM), F32)]
        + [jax.ShapeDtypeStruct(w.shape, BF16) for w in casts],
        scratch_shapes=[
            pltpu.VMEM((tq, width), F32),
            pltpu.VMEM((tq, width), F32),
            pltpu.VMEM((tq, heads * tk), F32),
            pltpu.VMEM((tq, heads * tk), BF16),
            pltpu.VMEM((tq, heads * tk), F32),
            pltpu.VMEM((tq, heads * tk), BF16),
        ],
        compiler_params=_params(("parallel", "parallel", "arbitrary")),
        name="stickbreak_attention",
    )(qkv, qkv, qkv, negu, *casts)
    return outs[0], outs[1:]


RSQRT_FLOOR = 1e-30
SCAN_UNROLL = 4


def _softplus(x):
    return jnp.maximum(x, 0.0) + jnp.log1p(jnp.exp(-jnp.abs(x)))


def _sigmoid(x):
    return 0.5 * (jnp.tanh(0.5 * x) + 1.0)


def _rglru_kernel(xr_ref, yr_ref, wc_ref, bc_ref, wa_ref, ba_ref, wx_ref, bx_ref, lam_ref,
                  o_ref, xext_ref, a_ref, b_ref, h_ref, *, ts, tw):
    si = pl.program_id(2)

    @pl.when(si == 0)
    def _():
        xext_ref[pl.ds(0, HALO), :] = jnp.zeros((HALO, tw), F32)
        h_ref[...] = jnp.zeros_like(h_ref)

    @pl.when(si != 0)
    def _():
        xext_ref[pl.ds(0, HALO), :] = xext_ref[pl.ds(ts, HALO), :]

    xext_ref[pl.ds(HALO, ts), :] = xr_ref[...]

    xc = bc_ref[...] + wc_ref[pl.ds(REC_CONV - 1, 1), :] * xr_ref[...]
    for k in range(REC_CONV - 1):
        shift = REC_CONV - 1 - k
        xc = xc + wc_ref[pl.ds(k, 1), :] * xext_ref[pl.ds(HALO - shift, ts), :]

    neg_c_sp = -RG_C * _softplus(-lam_ref[...])
    xc16 = xc.astype(BF16)
    for n in range(tw // V7X_LANES):
        cols = slice(n * V7X_LANES, (n + 1) * V7X_LANES)
        xb = xc16[:, cols]
        r = _sigmoid(jnp.dot(xb, wa_ref[n], preferred_element_type=F32) + ba_ref[:, cols])
        i = _sigmoid(jnp.dot(xb, wx_ref[n], preferred_element_type=F32) + bx_ref[:, cols])
        log_a = neg_c_sp[:, cols] * r
        a = jnp.exp(log_a)
        u = (1.0 - a) * (1.0 + a)
        mult = u * lax.rsqrt(jnp.maximum(u, RSQRT_FLOOR))
        a_ref[:, cols] = a
        b_ref[:, cols] = mult * (i * xc[:, cols])

    row = lax.broadcasted_iota(jnp.int32, (V7X_SUBLANES, tw), 0)

    def group(gi, h_prev):
        r0 = pl.multiple_of(gi * V7X_SUBLANES, V7X_SUBLANES)
        a = a_ref[pl.ds(r0, V7X_SUBLANES), :]
        b = b_ref[pl.ds(r0, V7X_SUBLANES), :]
        for d in (1, 2, 4):
            keep = row >= d
            a_sh = jnp.where(keep, pltpu.roll(a, d, 0), 1.0)
            b_sh = jnp.where(keep, pltpu.roll(b, d, 0), 0.0)
            b = a * b_sh + b
            a = a * a_sh
        h = a * h_prev + b
        b_ref[pl.ds(r0, V7X_SUBLANES), :] = h
        return jnp.broadcast_to(h[V7X_SUBLANES - 1:, :], (V7X_SUBLANES, tw))

    h_last = lax.fori_loop(0, ts // V7X_SUBLANES, group, h_ref[...], unroll=SCAN_UNROLL)
    h_ref[...] = h_last
    o_ref[...] = jax.nn.gelu(yr_ref[...]) * b_ref[...]


def _rglru(xy, w_rconv, b_rconv, w_rg_a, b_rg_a, w_rg_x, b_rg_x, lam, batch, seq, *, ts, tw):
    width = lam.shape[-1]
    nw = width // tw
    ns = seq // ts
    gb = tw // V7X_LANES
    row_spec = pl.BlockSpec((1, tw), lambda b, w, s: (0, w))
    kernel = functools.partial(_rglru_kernel, ts=ts, tw=tw)
    return pl.pallas_call(
        kernel,
        grid=(batch, nw, ns),
        in_specs=[
            pl.BlockSpec((ts, tw), lambda b, w, s: (b * ns + s, w)),
            pl.BlockSpec((ts, tw), lambda b, w, s: (b * ns + s, nw + w)),
            pl.BlockSpec((REC_CONV, tw), lambda b, w, s: (0, w)),
            row_spec,
            pl.BlockSpec((gb, V7X_LANES, V7X_LANES), lambda b, w, s: (w, 0, 0)),
            row_spec,
            pl.BlockSpec((gb, V7X_LANES, V7X_LANES), lambda b, w, s: (w, 0, 0)),
            row_spec,
            row_spec,
        ],
        out_specs=pl.BlockSpec((ts, tw), lambda b, w, s: (b * ns + s, w)),
        out_shape=jax.ShapeDtypeStruct((batch * seq, width), F32),
        scratch_shapes=[
            pltpu.VMEM((ts + HALO, tw), F32),
            pltpu.VMEM((ts, tw), F32),
            pltpu.VMEM((ts, tw), F32),
            pltpu.VMEM((V7X_SUBLANES, tw), F32),
        ],
        compiler_params=_params(("parallel", "parallel", "arbitrary")),
        name="rglru_branch",
    )(xy, xy, w_rconv, b_rconv.reshape(1, width), w_rg_a.astype(BF16), b_rg_a.reshape(1, width),
      w_rg_x.astype(BF16), b_rg_x.reshape(1, width), lam.reshape(1, width))


def _outproj_kernel(att_hbm, rec_hbm, ga_ref, gr_ref, w_ref, res_ref, o_ref,
                    att_ref, rec_ref, a_ref, stat_ref, att_sem, rec_sem):
    def norms():
        _norm_into(a_ref, 0, 0, att_ref, ga_ref, att_ref.shape[0], stat_ref)
        _norm_into(a_ref, 0, att_ref.shape[1], rec_ref, gr_ref, rec_ref.shape[0], stat_ref)

    _row_tile_prologue([(att_hbm, att_ref, att_sem), (rec_hbm, rec_ref, rec_sem)], norms)
    o_ref[...] = res_ref[...] + jnp.dot(a_ref[...], w_ref[...], preferred_element_type=F32)


def _outproj(att, rec, g_att, g_rec, w, res, *, tm, tn):
    m, wa = att.shape
    wr = rec.shape[1]
    n = w.shape[1]
    return pl.pallas_call(
        _outproj_kernel,
        grid=(m // tm, n // tn),
        in_specs=[
            pl.BlockSpec(memory_space=pl.ANY),
            pl.BlockSpec(memory_space=pl.ANY),
            pl.BlockSpec((1, wa), lambda i, j: (0, 0)),
            pl.BlockSpec((1, wr), lambda i, j: (0, 0)),
            pl.BlockSpec((wa + wr, tn), lambda i, j: (0, j)),
            pl.BlockSpec((tm, tn), lambda i, j: (i, j)),
        ],
        out_specs=pl.BlockSpec((tm, tn), lambda i, j: (i, j)),
        out_shape=jax.ShapeDtypeStruct((m, n), F32),
        scratch_shapes=[
            pltpu.VMEM((tm, wa), F32),
            pltpu.VMEM((tm, wr), F32),
            pltpu.VMEM((tm, wa + wr), BF16),
            pltpu.VMEM((tm, V7X_LANES), F32),
            pltpu.SemaphoreType.DMA(()),
            pltpu.SemaphoreType.DMA(()),
        ],
        compiler_params=_params(("arbitrary", "arbitrary")),
        name="out_projection",
    )(att, rec, g_att, g_rec, w, res)


def _ffn_up_kernel(h_hbm, halo_ref, g_ref, wg_ref, wu_ref, cg_ref, cu_ref, bg_ref, bu_ref,
                   o_ref, h_ref, a_ref, stat_ref, sem, *, tm, seq):
    def norms():
        _norm_into(a_ref, FFN_HALO, 0, h_ref, g_ref, tm, stat_ref)
        seq_start = (pl.program_id(0) * tm) % seq == 0
        halo = _rms_rows(halo_ref[...], g_ref[...])
        a_ref[pl.ds(0, FFN_HALO), :] = jnp.where(seq_start, 0.0, halo).astype(a_ref.dtype)

    _row_tile_prologue([(h_hbm, h_ref, sem)], norms)
    a = a_ref[...]

    def conv(w_ref, c_ref, b_ref):
        y = jnp.dot(a, w_ref[...], preferred_element_type=F32)
        out = b_ref[...] + c_ref[pl.ds(FF_CONV - 1, 1), :] * y[FFN_HALO:, :]
        for k in range(FF_CONV - 1):
            shift = FF_CONV - 1 - k
            out = out + c_ref[pl.ds(k, 1), :] * y[FFN_HALO - shift:FFN_HALO - shift + tm, :]
        return out

    gate = conv(wg_ref, cg_ref, bg_ref)
    up = conv(wu_ref, cu_ref, bu_ref)
    o_ref[...] = (jax.nn.gelu(gate) * up).astype(o_ref.dtype)


def _ffn_up(h, g, w_up, w_conv, b_conv, seq, *, tm, tf):
    m, d = h.shape
    f = w_up.shape[1] // 2
    nf = f // tf
    kernel = functools.partial(_ffn_up_kernel, tm=tm, seq=seq)
    halo_blocks = tm // FFN_HALO
    b_conv = b_conv.reshape(1, 2 * f)
    return pl.pallas_call(
        kernel,
        grid=(m // tm, nf),
        in_specs=[
            pl.BlockSpec(memory_space=pl.ANY),
            pl.BlockSpec((FFN_HALO, d), lambda i, j: (jnp.maximum(i * halo_blocks - 1, 0), 0)),
            pl.BlockSpec((1, d), lambda i, j: (0, 0)),
            pl.BlockSpec((d, tf), lambda i, j: (0, j)),
            pl.BlockSpec((d, tf), lambda i, j: (0, nf + j)),
            pl.BlockSpec((FF_CONV, tf), lambda i, j: (0, j)),
            pl.BlockSpec((FF_CONV, tf), lambda i, j: (0, nf + j)),
            pl.BlockSpec((1, tf), lambda i, j: (0, j)),
            pl.BlockSpec((1, tf), lambda i, j: (0, nf + j)),
        ],
        out_specs=pl.BlockSpec((tm, tf), lambda i, j: (i, j)),
        out_shape=jax.ShapeDtypeStruct((m, f), BF16),
        scratch_shapes=[
            pltpu.VMEM((tm, d), F32),
            pltpu.VMEM((FFN_HALO + tm, d), BF16),
            pltpu.VMEM((tm, V7X_LANES), F32),
            pltpu.SemaphoreType.DMA(()),
        ],
        compiler_params=_params(("arbitrary", "arbitrary")),
        name="ffn_up_conv_gate",
    )(h, h, g, w_up, w_up, w_conv, w_conv, b_conv, b_conv)


def _matmul_res_kernel(a_ref, w_ref, res_ref, o_ref):
    y = jnp.dot(a_ref[...], w_ref[...], preferred_element_type=F32)

    @pl.when(pl.program_id(2) == 0)
    def _():
        o_ref[...] = res_ref[...] + y

    @pl.when(pl.program_id(2) != 0)
    def _():
        o_ref[...] += y


def _matmul_res(a, w, res, *, tm, tn, tk):
    m, kdim = a.shape
    n = w.shape[1]
    return pl.pallas_call(
        _matmul_res_kernel,
        grid=(m // tm, n // tn, kdim // tk),
        in_specs=[
            pl.BlockSpec((tm, tk), lambda i, j, k: (i, k)),
            pl.BlockSpec((tk, tn), lambda i, j, k: (k, j)),
            pl.BlockSpec((tm, tn), lambda i, j, k: (i, j)),
        ],
        out_specs=pl.BlockSpec((tm, tn), lambda i, j, k: (i, j)),
        out_shape=jax.ShapeDtypeStruct((m, n), F32),
        compiler_params=_params(("parallel", "parallel", "arbitrary")),
        name="ffn_down_projection",
    )(a, w, res)


def _ple_kernel(h_hbm, g_ref, wg_ref, p_ref, wp_ref, res_ref, o_ref, h_ref, a_ref, stat_ref, sem):
    _row_tile_prologue(
        [(h_hbm, h_ref, sem)],
        lambda: _norm_into(a_ref, 0, 0, h_ref, g_ref, h_ref.shape[0], stat_ref))
    gate = _sigmoid(jnp.dot(a_ref[...], wg_ref[...], preferred_element_type=F32))
    emb = jnp.dot(p_ref[...].astype(BF16), wp_ref[...], preferred_element_type=F32)
    o_ref[...] = res_ref[...] + emb * gate


def _ple(h, g, w_gate, p, w_ple, *, tm, tn):
    m, d = h.shape
    n = w_gate.shape[1]
    pd = p.shape[1]
    return pl.pallas_call(
        _ple_kernel,
        grid=(m // tm, n // tn),
        in_specs=[
            pl.BlockSpec(memory_space=pl.ANY),
            pl.BlockSpec((1, d), lambda i, j: (0, 0)),
            pl.BlockSpec((d, tn), lambda i, j: (0, j)),
            pl.BlockSpec((tm, pd), lambda i, j: (i, 0)),
            pl.BlockSpec((pd, tn), lambda i, j: (0, j)),
            pl.BlockSpec((tm, tn), lambda i, j: (i, j)),
        ],
        out_specs=pl.BlockSpec((tm, tn), lambda i, j: (i, j)),
        out_shape=jax.ShapeDtypeStruct((m, n), F32),
        scratch_shapes=[
            pltpu.VMEM((tm, d), F32),
            pltpu.VMEM((tm, d), BF16),
            pltpu.VMEM((tm, V7X_LANES), F32),
            pltpu.SemaphoreType.DMA(()),
        ],
        compiler_params=_params(("arbitrary", "arbitrary")),
        name="ple_gate",
    )(h, g, w_gate, p, w_ple, h)


def _rmsnorm_kernel(x_ref, g_ref, o_ref):
    o_ref[...] = _rms_rows(x_ref[...], g_ref[...])


def _rmsnorm(x, g, *, tm):
    m, d = x.shape
    return pl.pallas_call(
        _rmsnorm_kernel,
        grid=(m // tm,),
        in_specs=[pl.BlockSpec((tm, d), lambda i: (i, 0)), pl.BlockSpec((1, d), lambda i: (0, 0))],
        out_specs=pl.BlockSpec((tm, d), lambda i: (i, 0)),
        out_shape=jax.ShapeDtypeStruct((m, d), F32),
        compiler_params=_params(("parallel",)),
        name="final_rmsnorm",
    )(x, g)


def kernel(x, p, g_mix, w_in, w_rconv, b_rconv, w_rg_a, b_rg_a, w_rg_x, b_rg_x, lam, g_att_out, g_rec_out, w_out, g_ffn, w_up, w_ffconv, b_ffconv, w_down, g_ple, w_ple, w_ple_gate, g_final):
    batch, seq, d_model = x.shape
    depth = w_in.shape[0]
    lru_width = lam.shape[-1]
    att_width = w_out.shape[1] - lru_width
    n_heads = att_width // HEAD_DIM
    m = batch * seq

    h = x.reshape(m, d_model)
    for l in range(depth):
        w_in_l = w_in[l].astype(BF16)
        qkv_scale = jnp.concatenate(
            [jnp.full((1, att_width), 1.0 / math.sqrt(HEAD_DIM), F32),
             jnp.ones((1, 2 * att_width), F32)], axis=1)
        g_mix_l = g_mix[l].reshape(1, d_model)
        qkv = _norm_matmul(h, g_mix_l, w_in_l, 0, 3 * att_width, qkv_scale, BF16,
                           tm=1024, tn=1024, name="in_projection_qkv")
        xy = _norm_matmul(h, g_mix_l, w_in_l, 3 * att_width, 2 * lru_width,
                          jnp.ones((1, 2 * lru_width), F32), F32,
                          tm=1024, tn=1024, name="in_projection_lru")
        att, (w_out_l, w_up_l, w_down_l, w_gate_l) = _attention(
            qkv, [w_out[l], w_up[l], w_down[l], w_ple_gate[l]], batch, seq, n_heads,
            tq=256, heads=4)
        rec = _rglru(xy, w_rconv[l], b_rconv[l], w_rg_a[l], b_rg_a[l], w_rg_x[l], b_rg_x[l],
                     lam[l], batch, seq, ts=512, tw=512)
        h = _outproj(att, rec, g_att_out[l].reshape(1, att_width),
                     g_rec_out[l].reshape(1, lru_width), w_out_l, h, tm=1024, tn=512)
        act = _ffn_up(h, g_ffn[l].reshape(1, d_model), w_up_l, w_ffconv[l],
                      b_ffconv[l], seq, tm=1024, tf=512)
        h = _matmul_res(act, w_down_l, h, tm=1024, tn=1024, tk=4096)
        h = _ple(h, g_ple[l].reshape(1, d_model), w_gate_l,
                 p[l].reshape(m, -1), w_ple[l].astype(BF16), tm=1024, tn=512)
    out = _rmsnorm(h, g_final.reshape(1, d_model), tm=256)
    return out.reshape(batch, seq, d_model)
```

```python
import functools
import math

import jax
import jax.numpy as jnp
from jax import lax
from jax.experimental import pallas as pl
from jax.experimental.pallas import tpu as pltpu

F32 = jnp.float32
BF16 = jnp.bfloat16

EPS = 1e-6
HEAD_DIM = 128
N_LRU_BLOCKS = 16
RG_C = 8.0
REC_CONV = 4
FF_CONV = 3

V7X_LANES = 128
V7X_SUBLANES = 8
V7X_VMEM_LIMIT_BYTES = 60000 * 1024

V7X_BF16_ROWS = 2 * V7X_SUBLANES

HALO = V7X_SUBLANES
FFN_HALO = V7X_BF16_ROWS
NORM_ROWS = 32


def _params(semantics, vmem_bytes=V7X_VMEM_LIMIT_BYTES):
    return pltpu.CompilerParams(dimension_semantics=semantics, vmem_limit_bytes=vmem_bytes)


def _rms_rows(x, g):
    ms = jnp.mean(x * x, axis=-1, keepdims=True)
    return x * lax.rsqrt(ms + EPS) * g


def _norm_into(dst_ref, dst_row0, dst_col0, src_ref, g_ref, rows, stat_ref):
    width = src_ref.shape[-1]
    lane_tiles = width // V7X_LANES
    chunk = min(NORM_ROWS, rows)

    def sum_squares(c, carry):
        r0 = pl.multiple_of(c * chunk, chunk)
        x = src_ref[pl.ds(r0, chunk), :]
        sq = x * x
        parts = [sq[:, k * V7X_LANES:(k + 1) * V7X_LANES] for k in range(lane_tiles)]
        while len(parts) > 1:
            parts = [a + b for a, b in zip(parts[0::2], parts[1::2])] + parts[len(parts) & ~1:]
        stat_ref[pl.ds(r0, chunk), :] = parts[0]
        return carry

    lax.fori_loop(0, rows // chunk, sum_squares, 0)
    ms = jnp.sum(stat_ref[pl.ds(0, rows), :], axis=-1, keepdims=True) * (1.0 / width)
    stat_ref[pl.ds(0, rows), :] = jnp.broadcast_to(lax.rsqrt(ms + EPS), (rows, V7X_LANES))

    def scale(c, carry):
        r0 = pl.multiple_of(c * chunk, chunk)
        rstd = jnp.concatenate([stat_ref[pl.ds(r0, chunk), :]] * lane_tiles, axis=1)
        y = src_ref[pl.ds(r0, chunk), :] * rstd * g_ref[...]
        dst_ref[pl.ds(dst_row0 + r0, chunk), pl.ds(dst_col0, width)] = y.astype(dst_ref.dtype)
        return carry

    lax.fori_loop(0, rows // chunk, scale, 0)


def _row_tile_prologue(sources, consume):
    i = pl.program_id(0)

    def copies(tile):
        return [pltpu.make_async_copy(hbm.at[pl.ds(tile * buf.shape[0], buf.shape[0]), :], buf, sem)
                for hbm, buf, sem in sources]

    @pl.when(pl.program_id(1) == 0)
    def _():
        @pl.when(i == 0)
        def _():
            for c in copies(0):
                c.start()

        for c in copies(i):
            c.wait()
        consume()

        @pl.when(i + 1 < pl.num_programs(0))
        def _():
            for c in copies(i + 1):
                c.start()


def _norm_matmul_kernel(x_hbm, g_ref, w_ref, cs_ref, o_ref, x_ref, a_ref, stat_ref, sem):
    _row_tile_prologue(
        [(x_hbm, x_ref, sem)],
        lambda: _norm_into(a_ref, 0, 0, x_ref, g_ref, x_ref.shape[0], stat_ref))
    y = jnp.dot(a_ref[...], w_ref[...], preferred_element_type=F32)
    o_ref[...] = (y * cs_ref[...]).astype(o_ref.dtype)


def _norm_matmul(x, g, w, col0, n, col_scale, out_dtype, *, tm, tn, name):
    m, d = x.shape
    j0 = col0 // tn
    return pl.pallas_call(
        _norm_matmul_kernel,
        grid=(m // tm, n // tn),
        in_specs=[
            pl.BlockSpec(memory_space=pl.ANY),
            pl.BlockSpec((1, d), lambda i, j: (0, 0)),
            pl.BlockSpec((d, tn), lambda i, j: (0, j0 + j)),
            pl.BlockSpec((1, tn), lambda i, j: (0, j)),
        ],
        out_specs=pl.BlockSpec((tm, tn), lambda i, j: (i, j)),
        out_shape=jax.ShapeDtypeStruct((m, n), out_dtype),
        scratch_shapes=[
            pltpu.VMEM((tm, d), F32),
            pltpu.VMEM((tm, d), BF16),
            pltpu.VMEM((tm, V7X_LANES), F32),
            pltpu.SemaphoreType.DMA(()),
        ],
        compiler_params=_params(("arbitrary", "arbitrary")),
        name=name,
    )(x, g, w, col_scale)


MASKED_SCORE = -1e30


def _attn_scores(q, k, diagonal):
    tq, tk = q.shape[0], k.shape[0]
    z = lax.dot_general(q, k, (((1,), (1,)), ((), ())), preferred_element_type=F32)
    sp = jnp.maximum(z, 0.0) + jnp.log(1.0 + jnp.exp(-jnp.abs(z)))
    if diagonal:
        row = lax.broadcasted_iota(jnp.int32, (tq, tk), 0)
        col = lax.broadcasted_iota(jnp.int32, (tq, tk), 1)
        mask = col < row
        sp = jnp.where(mask, sp, 0.0)
        z = jnp.where(mask, z, MASKED_SCORE)
    return z, sp.astype(BF16)


def _attn_accumulate(z, sp, v, negu_ref, acc_ref, csum_ref, cols):
    tk = z.shape[1]
    s = jnp.dot(sp, negu_ref[...], preferred_element_type=F32)
    csum = csum_ref[:, cols]
    a = jnp.exp(z + s + jnp.concatenate([csum] * (tk // V7X_LANES), axis=1))
    acc_ref[:, cols] += jnp.dot(a.astype(BF16), v, preferred_element_type=F32)
    csum_ref[:, cols] = csum + jnp.broadcast_to(s[:, :1], csum.shape)


def _attn_kernel(*refs, tq, tk, heads, n_casts):
    q_ref, k_ref, v_ref, negu_ref = refs[:4]
    cast_in = refs[4:4 + n_casts]
    o_ref = refs[4 + n_casts]
    cast_out = refs[5 + n_casts:5 + 2 * n_casts]
    acc_ref, csum_ref, z0_ref, sp0_ref, z1_ref, sp1_ref = refs[5 + 2 * n_casts:]
    qi = pl.program_id(2)
    acc_ref[...] = jnp.zeros_like(acc_ref)
    csum_ref[...] = jnp.zeros_like(csum_ref)
    for src, dst in zip(cast_in, cast_out):
        dst[...] = src[...].astype(dst.dtype)

    def key_rows(n):
        return pl.ds(pl.multiple_of((qi - n) * tk, tk), tk)

    def scores(n, z_ref, sp_ref, diagonal=False):
        rows = key_rows(n)
        for g in range(heads):
            cols = slice(g * HEAD_DIM, (g + 1) * HEAD_DIM)
            z, sp = _attn_scores(q_ref[:, cols], k_ref[rows, cols], diagonal)
            z_ref[:, g * tk:(g + 1) * tk] = z
            sp_ref[:, g * tk:(g + 1) * tk] = sp

    def accumulate(n, z_ref, sp_ref):
        rows = key_rows(n)
        for g in range(heads):
            cols = slice(g * HEAD_DIM, (g + 1) * HEAD_DIM)
            tile = slice(g * tk, (g + 1) * tk)
            _attn_accumulate(z_ref[:, tile], sp_ref[:, tile], v_ref[rows, cols], negu_ref,
                             acc_ref, csum_ref, cols)

    scores(0, z0_ref, sp0_ref, diagonal=True)

    def pair(p, carry):
        n = 2 * p
        accumulate(n, z0_ref, sp0_ref)
        scores(n + 1, z1_ref, sp1_ref)
        accumulate(n + 1, z1_ref, sp1_ref)
        scores(n + 2, z0_ref, sp0_ref)
        return carry

    lax.fori_loop(0, qi // 2, pair, 0)
    n_done = 2 * (qi // 2)

    @pl.when(qi % 2 == 1)
    def _():
        accumulate(n_done, z0_ref, sp0_ref)
        scores(n_done + 1, z1_ref, sp1_ref)
        accumulate(n_done + 1, z1_ref, sp1_ref)

    @pl.when(qi % 2 == 0)
    def _():
        accumulate(n_done, z0_ref, sp0_ref)

    o_ref[...] = acc_ref[...].astype(o_ref.dtype)


def _attention(qkv, casts, batch, seq, n_heads, *, tq, heads):
    tk = tq
    nq = seq // tq
    width = heads * HEAD_DIM
    groups = n_heads // heads
    steps = batch * groups * nq
    row = lax.broadcasted_iota(jnp.int32, (tk, tk), 0)
    col = lax.broadcasted_iota(jnp.int32, (tk, tk), 1)
    negu = -(row >= col).astype(BF16)
    kernel = functools.partial(_attn_kernel, tq=tq, tk=tk, heads=heads, n_casts=len(casts))
    for w in casts:
        assert w.shape[0] % (steps * V7X_BF16_ROWS) == 0, w.shape
    slab_specs = [pl.BlockSpec((w.shape[0] // steps, w.shape[1]),
                               lambda b, h, i: ((b * groups + h) * nq + i, 0)) for w in casts]
    outs = pl.pallas_call(
        kernel,
        grid=(batch, groups, nq),
        in_specs=[
            pl.BlockSpec((tq, width), lambda b, h, i: (b * nq + i, h)),
            pl.BlockSpec((seq, width), lambda b, h, i: (b, groups + h)),
            pl.BlockSpec((seq, width), lambda b, h, i: (b, 2 * groups + h)),
            pl.BlockSpec((tk, tk), lambda b, h, i: (0, 0)),
        ] + slab_specs,
        out_specs=[pl.BlockSpec((tq, width), lambda b, h, i: (b * nq + i, h))] + slab_specs,
        out_shape=[jax.ShapeDtypeStruct((batch * seq, n_heads * HEAD_DIM), F32)]
        + [jax.ShapeDtypeStruct(w.shape, BF16) for w in casts],
        scratch_shapes=[
            pltpu.VMEM((tq, width), F32),
            pltpu.VMEM((tq, width), F32),
            pltpu.VMEM((tq, heads * tk), F32),
            pltpu.VMEM((tq, heads * tk), BF16),
            pltpu.VMEM((tq, heads * tk), F32),
            pltpu.VMEM((tq, heads * tk), BF16),
        ],
        compiler_params=_params(("parallel", "parallel", "arbitrary")),
        name="stickbreak_attention",
    )(qkv, qkv, qkv, negu, *casts)
    return outs[0], outs[1:]


RSQRT_FLOOR = 1e-30
SCAN_UNROLL = 4


def _softplus(x):
    return jnp.maximum(x, 0.0) + jnp.log1p(jnp.exp(-jnp.abs(x)))


def _sigmoid(x):
    return 0.5 * (jnp.tanh(0.5 * x) + 1.0)


def _rglru_kernel(xr_ref, yr_ref, wc_ref, bc_ref, wa_ref, ba_ref, wx_ref, bx_ref, lam_ref,
                  o_ref, xext_ref, a_ref, b_ref, h_ref, *, ts, tw):
    si = pl.program_id(2)

    @pl.when(si == 0)
    def _():
        xext_ref[pl.ds(0, HALO), :] = jnp.zeros((HALO, tw), F32)
        h_ref[...] = jnp.zeros_like(h_ref)

    @pl.when(si != 0)
    def _():
        xext_ref[pl.ds(0, HALO), :] = xext_ref[pl.ds(ts, HALO), :]

    xext_ref[pl.ds(HALO, ts), :] = xr_ref[...]

    xc = bc_ref[...] + wc_ref[pl.ds(REC_CONV - 1, 1), :] * xr_ref[...]
    for k in range(REC_CONV - 1):
        shift = REC_CONV - 1 - k
        xc = xc + wc_ref[pl.ds(k, 1), :] * xext_ref[pl.ds(HALO - shift, ts), :]

    neg_c_sp = -RG_C * _softplus(-lam_ref[...])
    xc16 = xc.astype(BF16)
    for n in range(tw // V7X_LANES):
        cols = slice(n * V7X_LANES, (n + 1) * V7X_LANES)
        xb = xc16[:, cols]
        r = _sigmoid(jnp.dot(xb, wa_ref[n], preferred_element_type=F32) + ba_ref[:, cols])
        i = _sigmoid(jnp.dot(xb, wx_ref[n], preferred_element_type=F32) + bx_ref[:, cols])
        log_a = neg_c_sp[:, cols] * r
        a = jnp.exp(log_a)
        u = (1.0 - a) * (1.0 + a)
        mult = u * lax.rsqrt(jnp.maximum(u, RSQRT_FLOOR))
        a_ref[:, cols] = a
        b_ref[:, cols] = mult * (i * xc[:, cols])

    row = lax.broadcasted_iota(jnp.int32, (V7X_SUBLANES, tw), 0)

    def group(gi, h_prev):
        r0 = pl.multiple_of(gi * V7X_SUBLANES, V7X_SUBLANES)
        a = a_ref[pl.ds(r0, V7X_SUBLANES), :]
        b = b_ref[pl.ds(r0, V7X_SUBLANES), :]
        for d in (1, 2, 4):
            keep = row >= d
            a_sh = jnp.where(keep, pltpu.roll(a, d, 0), 1.0)
            b_sh = jnp.where(keep, pltpu.roll(b, d, 0), 0.0)
            b = a * b_sh + b
            a = a * a_sh
        h = a * h_prev + b
        b_ref[pl.ds(r0, V7X_SUBLANES), :] = h
        return jnp.broadcast_to(h[V7X_SUBLANES - 1:, :], (V7X_SUBLANES, tw))

    h_last = lax.fori_loop(0, ts // V7X_SUBLANES, group, h_ref[...], unroll=SCAN_UNROLL)
    h_ref[...] = h_last
    o_ref[...] = jax.nn.gelu(yr_ref[...]) * b_ref[...]


def _rglru(xy, w_rconv, b_rconv, w_rg_a, b_rg_a, w_rg_x, b_rg_x, lam, batch, seq, *, ts, tw):
    width = lam.shape[-1]
    nw = width // tw
    ns = seq // ts
    gb = tw // V7X_LANES
    row_spec = pl.BlockSpec((1, tw), lambda b, w, s: (0, w))
    kernel = functools.partial(_rglru_kernel, ts=ts, tw=tw)
    return pl.pallas_call(
        kernel,
        grid=(batch, nw, ns),
        in_specs=[
            pl.BlockSpec((ts, tw), lambda b, w, s: (b * ns + s, w)),
            pl.BlockSpec((ts, tw), lambda b, w, s: (b * ns + s, nw + w)),
            pl.BlockSpec((REC_CONV, tw), lambda b, w, s: (0, w)),
            row_spec,
            pl.BlockSpec((gb, V7X_LANES, V7X_LANES), lambda b, w, s: (w, 0, 0)),
            row_spec,
            pl.BlockSpec((gb, V7X_LANES, V7X_LANES), lambda b, w, s: (w, 0, 0)),
            row_spec,
            row_spec,
        ],
        out_specs=pl.BlockSpec((ts, tw), lambda b, w, s: (b * ns + s, w)),
        out_shape=jax.ShapeDtypeStruct((batch * seq, width), F32),
        scratch_shapes=[
            pltpu.VMEM((ts + HALO, tw), F32),
            pltpu.VMEM((ts, tw), F32),
            pltpu.VMEM((ts, tw), F32),
            pltpu.VMEM((V7X_SUBLANES, tw), F32),
        ],
        compiler_params=_params(("parallel", "parallel", "arbitrary")),
        name="rglru_branch",
    )(xy, xy, w_rconv, b_rconv.reshape(1, width), w_rg_a.astype(BF16), b_rg_a.reshape(1, width),
      w_rg_x.astype(BF16), b_rg_x.reshape(1, width), lam.reshape(1, width))


def _outproj_kernel(att_hbm, rec_hbm, ga_ref, gr_ref, w_ref, res_ref, o_ref,
                    att_ref, rec_ref, a_ref, stat_ref, att_sem, rec_sem):
    def norms():
        _norm_into(a_ref, 0, 0, att_ref, ga_ref, att_ref.shape[0], stat_ref)
        _norm_into(a_ref, 0, att_ref.shape[1], rec_ref, gr_ref, rec_ref.shape[0], stat_ref)

    _row_tile_prologue([(att_hbm, att_ref, att_sem), (rec_hbm, rec_ref, rec_sem)], norms)
    o_ref[...] = res_ref[...] + jnp.dot(a_ref[...], w_ref[...], preferred_element_type=F32)


def _outproj(att, rec, g_att, g_rec, w, res, *, tm, tn):
    m, wa = att.shape
    wr = rec.shape[1]
    n = w.shape[1]
    return pl.pallas_call(
        _outproj_kernel,
        grid=(m // tm, n // tn),
        in_specs=[
            pl.BlockSpec(memory_space=pl.ANY),
            pl.BlockSpec(memory_space=pl.ANY),
            pl.BlockSpec((1, wa), lambda i, j: (0, 0)),
            pl.BlockSpec((1, wr), lambda i, j: (0, 0)),
            pl.BlockSpec((wa + wr, tn), lambda i, j: (0, j)),
            pl.BlockSpec((tm, tn), lambda i, j: (i, j)),
        ],
        out_specs=pl.BlockSpec((tm, tn), lambda i, j: (i, j)),
        out_shape=jax.ShapeDtypeStruct((m, n), F32),
        scratch_shapes=[
            pltpu.VMEM((tm, wa), F32),
            pltpu.VMEM((tm, wr), F32),
            pltpu.VMEM((tm, wa + wr), BF16),
            pltpu.VMEM((tm, V7X_LANES), F32),
            pltpu.SemaphoreType.DMA(()),
            pltpu.SemaphoreType.DMA(()),
        ],
        compiler_params=_params(("arbitrary", "arbitrary")),
        name="out_projection",
    )(att, rec, g_att, g_rec, w, res)


def _ffn_up_kernel(h_hbm, halo_ref, g_ref, wg_ref, wu_ref, cg_ref, cu_ref, bg_ref, bu_ref,
                   o_ref, h_ref, a_ref, stat_ref, sem, *, tm, seq):
    def norms():
        _norm_into(a_ref, FFN_HALO, 0, h_ref, g_ref, tm, stat_ref)
        seq_start = (pl.program_id(0) * tm) % seq == 0
        halo = _rms_rows(halo_ref[...], g_ref[...])
        a_ref[pl.ds(0, FFN_HALO), :] = jnp.where(seq_start, 0.0, halo).astype(a_ref.dtype)

    _row_tile_prologue([(h_hbm, h_ref, sem)], norms)
    a = a_ref[...]

    def conv(w_ref, c_ref, b_ref):
        y = jnp.dot(a, w_ref[...], preferred_element_type=F32)
        out = b_ref[...] + c_ref[pl.ds(FF_CONV - 1, 1), :] * y[FFN_HALO:, :]
        for k in range(FF_CONV - 1):
            shift = FF_CONV - 1 - k
            out = out + c_ref[pl.ds(k, 1), :] * y[FFN_HALO - shift:FFN_HALO - shift + tm, :]
        return out

    gate = conv(wg_ref, cg_ref, bg_ref)
    up = conv(wu_ref, cu_ref, bu_ref)
    o_ref[...] = (jax.nn.gelu(gate) * up).astype(o_ref.dtype)


def _ffn_up(h, g, w_up, w_conv, b_conv, seq, *, tm, tf):
    m, d = h.shape
    f = w_up.shape[1] // 2
    nf = f // tf
    kernel = functools.partial(_ffn_up_kernel, tm=tm, seq=seq)
    halo_blocks = tm // FFN_HALO
    b_conv = b_conv.reshape(1, 2 * f)
    return pl.pallas_call(
        kernel,
        grid=(m // tm, nf),
        in_specs=[
            pl.BlockSpec(memory_space=pl.ANY),
            pl.BlockSpec((FFN_HALO, d), lambda i, j: (jnp.maximum(i * halo_blocks - 1, 0), 0)),
            pl.BlockSpec((1, d), lambda i, j: (0, 0)),
            pl.BlockSpec((d, tf), lambda i, j: (0, j)),
            pl.BlockSpec((d, tf), lambda i, j: (0, nf + j)),
            pl.BlockSpec((FF_CONV, tf), lambda i, j: (0, j)),
            pl.BlockSpec((FF_CONV, tf), lambda i, j: (0, nf + j)),
            pl.BlockSpec((1, tf), lambda i, j: (0, j)),
            pl.BlockSpec((1, tf), lambda i, j: (0, nf + j)),
        ],
        out_specs=pl.BlockSpec((tm, tf), lambda i, j: (i, j)),
        out_shape=jax.ShapeDtypeStruct((m, f), BF16),
        scratch_shapes=[
            pltpu.VMEM((tm, d), F32),
            pltpu.VMEM((FFN_HALO + tm, d), BF16),
            pltpu.VMEM((tm, V7X_LANES), F32),
            pltpu.SemaphoreType.DMA(()),
        ],
        compiler_params=_params(("arbitrary", "arbitrary")),
        name="ffn_up_conv_gate",
    )(h, h, g, w_up, w_up, w_conv, w_conv, b_conv, b_conv)


def _matmul_res_kernel(a_ref, w_ref, res_ref, o_ref):
    @pl.when(pl.program_id(2) == 0)
    def _():
        o_ref[...] = res_ref[...]

    o_ref[...] += jnp.dot(a_ref[...], w_ref[...], preferred_element_type=F32)


def _matmul_res(a, w, res, *, tm, tn, tk):
    m, kdim = a.shape
    n = w.shape[1]
    return pl.pallas_call(
        _matmul_res_kernel,
        grid=(m // tm, n // tn, kdim // tk),
        in_specs=[
            pl.BlockSpec((tm, tk), lambda i, j, k: (i, k)),
            pl.BlockSpec((tk, tn), lambda i, j, k: (k, j)),
            pl.BlockSpec((tm, tn), lambda i, j, k: (i, j)),
        ],
        out_specs=pl.BlockSpec((tm, tn), lambda i, j, k: (i, j)),
        out_shape=jax.ShapeDtypeStruct((m, n), F32),
        compiler_params=_params(("parallel", "parallel", "arbitrary")),
        name="ffn_down_projection",
    )(a, w, res)


def _ple_kernel(h_hbm, g_ref, wg_ref, p_ref, wp_ref, res_ref, o_ref, h_ref, a_ref, stat_ref, sem):
    _row_tile_prologue(
        [(h_hbm, h_ref, sem)],
        lambda: _norm_into(a_ref, 0, 0, h_ref, g_ref, h_ref.shape[0], stat_ref))
    gate = _sigmoid(jnp.dot(a_ref[...], wg_ref[...], preferred_element_type=F32))
    emb = jnp.dot(p_ref[...].astype(BF16), wp_ref[...], preferred_element_type=F32)
    o_ref[...] = res_ref[...] + emb * gate


def _ple(h, g, w_gate, p, w_ple, *, tm, tn):
    m, d = h.shape
    n = w_gate.shape[1]
    pd = p.shape[1]
    return pl.pallas_call(
        _ple_kernel,
        grid=(m // tm, n // tn),
        in_specs=[
            pl.BlockSpec(memory_space=pl.ANY),
            pl.BlockSpec((1, d), lambda i, j: (0, 0)),
            pl.BlockSpec((d, tn), lambda i, j: (0, j)),
            pl.BlockSpec((tm, pd), lambda i, j: (i, 0)),
            pl.BlockSpec((pd, tn), lambda i, j: (0, j)),
            pl.BlockSpec((tm, tn), lambda i, j: (i, j)),
        ],
        out_specs=pl.BlockSpec((tm, tn), lambda i, j: (i, j)),
        out_shape=jax.ShapeDtypeStruct((m, n), F32),
        scratch_shapes=[
            pltpu.VMEM((tm, d), F32),
            pltpu.VMEM((tm, d), BF16),
            pltpu.VMEM((tm, V7X_LANES), F32),
            pltpu.SemaphoreType.DMA(()),
        ],
        compiler_params=_params(("arbitrary", "arbitrary")),
        name="ple_gate",
    )(h, g, w_gate, p, w_ple, h)


def _rmsnorm_kernel(x_ref, g_ref, o_ref):
    o_ref[...] = _rms_rows(x_ref[...], g_ref[...])


def _rmsnorm(x, g, *, tm):
    m, d = x.shape
    return pl.pallas_call(
        _rmsnorm_kernel,
        grid=(m // tm,),
        in_specs=[pl.BlockSpec((tm, d), lambda i: (i, 0)), pl.BlockSpec((1, d), lambda i: (0, 0))],
        out_specs=pl.BlockSpec((tm, d), lambda i: (i, 0)),
        out_shape=jax.ShapeDtypeStruct((m, d), F32),
        compiler_params=_params(("parallel",)),
        name="final_rmsnorm",
    )(x, g)


def kernel(x, p, g_mix, w_in, w_rconv, b_rconv, w_rg_a, b_rg_a, w_rg_x, b_rg_x, lam, g_att_out, g_rec_out, w_out, g_ffn, w_up, w_ffconv, b_ffconv, w_down, g_ple, w_ple, w_ple_gate, g_final):
    batch, seq, d_model = x.shape
    depth = w_in.shape[0]
    lru_width = lam.shape[-1]
    att_width = w_out.shape[1] - lru_width
    n_heads = att_width // HEAD_DIM
    m = batch * seq

    h = x.reshape(m, d_model)
    for l in range(depth):
        w_in_l = w_in[l].astype(BF16)
        qkv_scale = jnp.concatenate(
            [jnp.full((1, att_width), 1.0 / math.sqrt(HEAD_DIM), F32),
             jnp.ones((1, 2 * att_width), F32)], axis=1)
        g_mix_l = g_mix[l].reshape(1, d_model)
        qkv = _norm_matmul(h, g_mix_l, w_in_l, 0, 3 * att_width, qkv_scale, BF16,
                           tm=1024, tn=1024, name="in_projection_qkv")
        xy = _norm_matmul(h, g_mix_l, w_in_l, 3 * att_width, 2 * lru_width,
                          jnp.ones((1, 2 * lru_width), F32), F32,
                          tm=1024, tn=1024, name="in_projection_lru")
        att, (w_out_l, w_up_l, w_down_l, w_gate_l) = _attention(
            qkv, [w_out[l], w_up[l], w_down[l], w_ple_gate[l]], batch, seq, n_heads,
            tq=256, heads=4)
        rec = _rglru(xy, w_rconv[l], b_rconv[l], w_rg_a[l], b_rg_a[l], w_rg_x[l], b_rg_x[l],
                     lam[l], batch, seq, ts=512, tw=512)
        h = _outproj(att, rec, g_att_out[l].reshape(1, att_width),
                     g_rec_out[l].reshape(1, lru_width), w_out_l, h, tm=1024, tn=512)
        act = _ffn_up(h, g_ffn[l].reshape(1, d_model), w_up_l, w_ffconv[l],
                      b_ffconv[l], seq, tm=1024, tf=512)
        h = _matmul_res(act, w_down_l, h, tm=1024, tn=1024, tk=4096)
        h = _ple(h, g_ple[l].reshape(1, d_model), w_gate_l,
                 p[l].reshape(m, -1), w_ple[l].astype(BF16), tm=1024, tn=512)
    out = _rmsnorm(h, g_final.reshape(1, d_model), tm=256)
    return out.reshape(batch, seq, d_model)
```

```python
import functools
import math

import jax
import jax.numpy as jnp
from jax import lax
from jax.experimental import pallas as pl
from jax.experimental.pallas import tpu as pltpu

F32 = jnp.float32
BF16 = jnp.bfloat16

EPS = 1e-6
HEAD_DIM = 128
N_LRU_BLOCKS = 16
RG_C = 8.0
REC_CONV = 4
FF_CONV = 3

V7X_LANES = 128
V7X_SUBLANES = 8
V7X_VMEM_LIMIT_BYTES = 60000 * 1024

V7X_BF16_ROWS = 2 * V7X_SUBLANES

HALO = V7X_SUBLANES
FFN_HALO = V7X_BF16_ROWS
NORM_ROWS = 32


def _params(semantics, vmem_bytes=V7X_VMEM_LIMIT_BYTES):
    return pltpu.CompilerParams(dimension_semantics=semantics, vmem_limit_bytes=vmem_bytes)


def _rms_rows(x, g):
    ms = jnp.mean(x * x, axis=-1, keepdims=True)
    return x * lax.rsqrt(ms + EPS) * g


def _norm_into(dst_ref, dst_row0, dst_col0, src_ref, g_ref, rows, stat_ref):
    width = src_ref.shape[-1]
    lane_tiles = width // V7X_LANES
    chunk = min(NORM_ROWS, rows)

    def sum_squares(c, carry):
        r0 = pl.multiple_of(c * chunk, chunk)
        x = src_ref[pl.ds(r0, chunk), :]
        sq = x * x
        parts = [sq[:, k * V7X_LANES:(k + 1) * V7X_LANES] for k in range(lane_tiles)]
        while len(parts) > 1:
            parts = [a + b for a, b in zip(parts[0::2], parts[1::2])] + parts[len(parts) & ~1:]
        stat_ref[pl.ds(r0, chunk), :] = parts[0]
        return carry

    lax.fori_loop(0, rows // chunk, sum_squares, 0)
    ms = jnp.sum(stat_ref[pl.ds(0, rows), :], axis=-1, keepdims=True) * (1.0 / width)
    stat_ref[pl.ds(0, rows), :] = jnp.broadcast_to(lax.rsqrt(ms + EPS), (rows, V7X_LANES))

    def scale(c, carry):
        r0 = pl.multiple_of(c * chunk, chunk)
        rstd = jnp.concatenate([stat_ref[pl.ds(r0, chunk), :]] * lane_tiles, axis=1)
        y = src_ref[pl.ds(r0, chunk), :] * rstd * g_ref[...]
        dst_ref[pl.ds(dst_row0 + r0, chunk), pl.ds(dst_col0, width)] = y.astype(dst_ref.dtype)
        return carry

    lax.fori_loop(0, rows // chunk, scale, 0)


def _row_tile_prologue(sources, consume):
    i = pl.program_id(0)

    def copies(tile):
        return [pltpu.make_async_copy(hbm.at[pl.ds(tile * buf.shape[0], buf.shape[0]), :], buf, sem)
                for hbm, buf, sem in sources]

    @pl.when(pl.program_id(1) == 0)
    def _():
        @pl.when(i == 0)
        def _():
            for c in copies(0):
                c.start()

        for c in copies(i):
            c.wait()
        consume()

        @pl.when(i + 1 < pl.num_programs(0))
        def _():
            for c in copies(i + 1):
                c.start()


def _norm_matmul_kernel(x_hbm, g_ref, w_ref, cs_ref, o_ref, x_ref, a_ref, stat_ref, sem):
    _row_tile_prologue(
        [(x_hbm, x_ref, sem)],
        lambda: _norm_into(a_ref, 0, 0, x_ref, g_ref, x_ref.shape[0], stat_ref))
    y = jnp.dot(a_ref[...], w_ref[...], preferred_element_type=F32)
    o_ref[...] = (y * cs_ref[...]).astype(o_ref.dtype)


def _norm_matmul(x, g, w, col0, n, col_scale, out_dtype, *, tm, tn, name):
    m, d = x.shape
    j0 = col0 // tn
    return pl.pallas_call(
        _norm_matmul_kernel,
        grid=(m // tm, n // tn),
        in_specs=[
            pl.BlockSpec(memory_space=pl.ANY),
            pl.BlockSpec((1, d), lambda i, j: (0, 0)),
            pl.BlockSpec((d, tn), lambda i, j: (0, j0 + j)),
            pl.BlockSpec((1, tn), lambda i, j: (0, j)),
        ],
        out_specs=pl.BlockSpec((tm, tn), lambda i, j: (i, j)),
        out_shape=jax.ShapeDtypeStruct((m, n), out_dtype),
        scratch_shapes=[
            pltpu.VMEM((tm, d), F32),
            pltpu.VMEM((tm, d), BF16),
            pltpu.VMEM((tm, V7X_LANES), F32),
            pltpu.SemaphoreType.DMA(()),
        ],
        compiler_params=_params(("arbitrary", "arbitrary")),
        name=name,
    )(x, g, w, col_scale)


MASKED_SCORE = -1e30


def _attn_scores(q, k, diagonal):
    tq, tk = q.shape[0], k.shape[0]
    z = lax.dot_general(q, k, (((1,), (1,)), ((), ())), preferred_element_type=F32)
    sp = jnp.maximum(z, 0.0) + jnp.log(1.0 + jnp.exp(-jnp.abs(z)))
    if diagonal:
        row = lax.broadcasted_iota(jnp.int32, (tq, tk), 0)
        col = lax.broadcasted_iota(jnp.int32, (tq, tk), 1)
        mask = col < row
        sp = jnp.where(mask, sp, 0.0)
        z = jnp.where(mask, z, MASKED_SCORE)
    return z, sp.astype(BF16)


def _attn_accumulate(z, s, v, acc_ref, csum_ref, cols):
    tk = z.shape[1]
    csum = csum_ref[:, cols]
    a = jnp.exp(z + s + jnp.concatenate([csum] * (tk // V7X_LANES), axis=1))
    acc_ref[:, cols] += jnp.dot(a.astype(BF16), v, preferred_element_type=F32)
    csum_ref[:, cols] = csum + jnp.broadcast_to(s[:, :1], csum.shape)


def _attn_kernel(*refs, tq, tk, heads, n_casts):
    q_ref, k_ref, v_ref, negu_ref = refs[:4]
    cast_in = refs[4:4 + n_casts]
    o_ref = refs[4 + n_casts]
    cast_out = refs[5 + n_casts:5 + 2 * n_casts]
    acc_ref, csum_ref, z0_ref, sp0_ref, z1_ref, sp1_ref = refs[5 + 2 * n_casts:]
    qi = pl.program_id(2)
    acc_ref[...] = jnp.zeros_like(acc_ref)
    csum_ref[...] = jnp.zeros_like(csum_ref)

    def convert_slabs():
        for src, dst in zip(cast_in, cast_out):
            dst[...] = src[...].astype(dst.dtype)

    def key_rows(n):
        return pl.ds(pl.multiple_of((qi - n) * tk, tk), tk)

    def scores(n, z_ref, sp_ref, diagonal=False):
        rows = key_rows(n)
        for g in range(heads):
            cols = slice(g * HEAD_DIM, (g + 1) * HEAD_DIM)
            z, sp = _attn_scores(q_ref[:, cols], k_ref[rows, cols], diagonal)
            z_ref[g * tq:(g + 1) * tq, :] = z
            sp_ref[g * tq:(g + 1) * tq, :] = sp

    def accumulate(n, z_ref, sp_ref):
        rows = key_rows(n)
        s_all = jnp.dot(sp_ref[...], negu_ref[...], preferred_element_type=F32)
        for g in range(heads):
            cols = slice(g * HEAD_DIM, (g + 1) * HEAD_DIM)
            tile = slice(g * tq, (g + 1) * tq)
            _attn_accumulate(z_ref[tile, :], s_all[tile, :], v_ref[rows, cols],
                             acc_ref, csum_ref, cols)

    scores(0, z0_ref, sp0_ref, diagonal=True)

    def pair(p, carry):
        n = 2 * p
        accumulate(n, z0_ref, sp0_ref)
        scores(n + 1, z1_ref, sp1_ref)
        accumulate(n + 1, z1_ref, sp1_ref)
        scores(n + 2, z0_ref, sp0_ref)
        return carry

    lax.fori_loop(0, qi // 2, pair, 0)
    n_done = 2 * (qi // 2)

    @pl.when(qi % 2 == 1)
    def _():
        accumulate(n_done, z0_ref, sp0_ref)
        scores(n_done + 1, z1_ref, sp1_ref)
        accumulate(n_done + 1, z1_ref, sp1_ref)
        convert_slabs()

    @pl.when(qi % 2 == 0)
    def _():
        accumulate(n_done, z0_ref, sp0_ref)
        convert_slabs()

    o_ref[...] = acc_ref[...].astype(o_ref.dtype)


def _attention(qkv, casts, batch, seq, n_heads, *, tq, heads):
    tk = tq
    nq = seq // tq
    width = heads * HEAD_DIM
    groups = n_heads // heads
    steps = batch * groups * nq
    row = lax.broadcasted_iota(jnp.int32, (tk, tk), 0)
    col = lax.broadcasted_iota(jnp.int32, (tk, tk), 1)
    negu = -(row >= col).astype(BF16)
    kernel = functools.partial(_attn_kernel, tq=tq, tk=tk, heads=heads, n_casts=len(casts))
    for w in casts:
        assert w.shape[0] % (steps * V7X_BF16_ROWS) == 0, w.shape
    slab_specs = [pl.BlockSpec((w.shape[0] // steps, w.shape[1]),
                               lambda b, h, i: ((b * groups + h) * nq + i, 0)) for w in casts]
    outs = pl.pallas_call(
        kernel,
        grid=(batch, groups, nq),
        in_specs=[
            pl.BlockSpec((tq, width), lambda b, h, i: (b * nq + i, h)),
            pl.BlockSpec((seq, width), lambda b, h, i: (b, groups + h)),
            pl.BlockSpec((seq, width), lambda b, h, i: (b, 2 * groups + h)),
            pl.BlockSpec((tk, tk), lambda b, h, i: (0, 0)),
        ] + slab_specs,
        out_specs=[pl.BlockSpec((tq, width), lambda b, h, i: (b * nq + i, h))] + slab_specs,
        out_shape=[jax.ShapeDtypeStruct((batch * seq, n_heads * HEAD_DIM), F32)]
        + [jax.ShapeDtypeStruct(w.shape, BF16) for w in casts],
        scratch_shapes=[
            pltpu.VMEM((tq, width), F32),
            pltpu.VMEM((tq, width), F32),
            pltpu.VMEM((heads * tq, tk), F32),
            pltpu.VMEM((heads * tq, tk), BF16),
            pltpu.VMEM((heads * tq, tk), F32),
            pltpu.VMEM((heads * tq, tk), BF16),
        ],
        compiler_params=_params(("parallel", "parallel", "arbitrary")),
        name="stickbreak_attention",
    )(qkv, qkv, qkv, negu, *casts)
    return outs[0], outs[1:]


RSQRT_FLOOR = 1e-30
SCAN_UNROLL = 4


def _softplus(x):
    return jnp.maximum(x, 0.0) + jnp.log1p(jnp.exp(-jnp.abs(x)))


def _sigmoid(x):
    return 0.5 * (jnp.tanh(0.5 * x) + 1.0)


def _rglru_kernel(xr_ref, yr_ref, wc_ref, bc_ref, wa_ref, ba_ref, wx_ref, bx_ref, lam_ref,
                  o_ref, xext_ref, a_ref, b_ref, h_ref, *, ts, tw):
    si = pl.program_id(2)

    @pl.when(si == 0)
    def _():
        xext_ref[pl.ds(0, HALO), :] = jnp.zeros((HALO, tw), F32)
        h_ref[...] = jnp.zeros_like(h_ref)

    @pl.when(si != 0)
    def _():
        xext_ref[pl.ds(0, HALO), :] = xext_ref[pl.ds(ts, HALO), :]

    xext_ref[pl.ds(HALO, ts), :] = xr_ref[...]

    xc = bc_ref[...] + wc_ref[pl.ds(REC_CONV - 1, 1), :] * xr_ref[...]
    for k in range(REC_CONV - 1):
        shift = REC_CONV - 1 - k
        xc = xc + wc_ref[pl.ds(k, 1), :] * xext_ref[pl.ds(HALO - shift, ts), :]

    neg_c_sp = -RG_C * _softplus(-lam_ref[...])
    xc16 = xc.astype(BF16)
    for n in range(tw // V7X_LANES):
        cols = slice(n * V7X_LANES, (n + 1) * V7X_LANES)
        xb = xc16[:, cols]
        r = _sigmoid(jnp.dot(xb, wa_ref[n], preferred_element_type=F32) + ba_ref[:, cols])
        i = _sigmoid(jnp.dot(xb, wx_ref[n], preferred_element_type=F32) + bx_ref[:, cols])
        log_a = neg_c_sp[:, cols] * r
        a = jnp.exp(log_a)
        u = (1.0 - a) * (1.0 + a)
        mult = u * lax.rsqrt(jnp.maximum(u, RSQRT_FLOOR))
        a_ref[:, cols] = a
        b_ref[:, cols] = mult * (i * xc[:, cols])

    row = lax.broadcasted_iota(jnp.int32, (V7X_SUBLANES, tw), 0)

    def group(gi, h_prev):
        r0 = pl.multiple_of(gi * V7X_SUBLANES, V7X_SUBLANES)
        a = a_ref[pl.ds(r0, V7X_SUBLANES), :]
        b = b_ref[pl.ds(r0, V7X_SUBLANES), :]
        for d in (1, 2, 4):
            keep = row >= d
            a_sh = jnp.where(keep, pltpu.roll(a, d, 0), 1.0)
            b_sh = jnp.where(keep, pltpu.roll(b, d, 0), 0.0)
            b = a * b_sh + b
            a = a * a_sh
        h = a * h_prev + b
        b_ref[pl.ds(r0, V7X_SUBLANES), :] = h
        return jnp.broadcast_to(h[V7X_SUBLANES - 1:, :], (V7X_SUBLANES, tw))

    h_last = lax.fori_loop(0, ts // V7X_SUBLANES, group, h_ref[...], unroll=SCAN_UNROLL)
    h_ref[...] = h_last
    o_ref[...] = jax.nn.gelu(yr_ref[...]) * b_ref[...]


def _rglru(xy, w_rconv, b_rconv, w_rg_a, b_rg_a, w_rg_x, b_rg_x, lam, batch, seq, *, ts, tw):
    width = lam.shape[-1]
    nw = width // tw
    ns = seq // ts
    gb = tw // V7X_LANES
    row_spec = pl.BlockSpec((1, tw), lambda b, w, s: (0, w))
    kernel = functools.partial(_rglru_kernel, ts=ts, tw=tw)
    return pl.pallas_call(
        kernel,
        grid=(batch, nw, ns),
        in_specs=[
            pl.BlockSpec((ts, tw), lambda b, w, s: (b * ns + s, w)),
            pl.BlockSpec((ts, tw), lambda b, w, s: (b * ns + s, nw + w)),
            pl.BlockSpec((REC_CONV, tw), lambda b, w, s: (0, w)),
            row_spec,
            pl.BlockSpec((gb, V7X_LANES, V7X_LANES), lambda b, w, s: (w, 0, 0)),
            row_spec,
            pl.BlockSpec((gb, V7X_LANES, V7X_LANES), lambda b, w, s: (w, 0, 0)),
            row_spec,
            row_spec,
        ],
        out_specs=pl.BlockSpec((ts, tw), lambda b, w, s: (b * ns + s, w)),
        out_shape=jax.ShapeDtypeStruct((batch * seq, width), F32),
        scratch_shapes=[
            pltpu.VMEM((ts + HALO, tw), F32),
            pltpu.VMEM((ts, tw), F32),
            pltpu.VMEM((ts, tw), F32),
            pltpu.VMEM((V7X_SUBLANES, tw), F32),
        ],
        compiler_params=_params(("parallel", "parallel", "arbitrary")),
        name="rglru_branch",
    )(xy, xy, w_rconv, b_rconv.reshape(1, width), w_rg_a.astype(BF16), b_rg_a.reshape(1, width),
      w_rg_x.astype(BF16), b_rg_x.reshape(1, width), lam.reshape(1, width))


def _outproj_kernel(att_hbm, rec_hbm, ga_ref, gr_ref, w_ref, res_ref, o_ref,
                    att_ref, rec_ref, a_ref, stat_ref, att_sem, rec_sem):
    def norms():
        _norm_into(a_ref, 0, 0, att_ref, ga_ref, att_ref.shape[0], stat_ref)
        _norm_into(a_ref, 0, att_ref.shape[1], rec_ref, gr_ref, rec_ref.shape[0], stat_ref)

    _row_tile_prologue([(att_hbm, att_ref, att_sem), (rec_hbm, rec_ref, rec_sem)], norms)
    o_ref[...] = res_ref[...] + jnp.dot(a_ref[...], w_ref[...], preferred_element_type=F32)


def _outproj(att, rec, g_att, g_rec, w, res, *, tm, tn):
    m, wa = att.shape
    wr = rec.shape[1]
    n = w.shape[1]
    return pl.pallas_call(
        _outproj_kernel,
        grid=(m // tm, n // tn),
        in_specs=[
            pl.BlockSpec(memory_space=pl.ANY),
            pl.BlockSpec(memory_space=pl.ANY),
            pl.BlockSpec((1, wa), lambda i, j: (0, 0)),
            pl.BlockSpec((1, wr), lambda i, j: (0, 0)),
            pl.BlockSpec((wa + wr, tn), lambda i, j: (0, j)),
            pl.BlockSpec((tm, tn), lambda i, j: (i, j)),
        ],
        out_specs=pl.BlockSpec((tm, tn), lambda i, j: (i, j)),
        out_shape=jax.ShapeDtypeStruct((m, n), F32),
        scratch_shapes=[
            pltpu.VMEM((tm, wa), F32),
            pltpu.VMEM((tm, wr), F32),
            pltpu.VMEM((tm, wa + wr), BF16),
            pltpu.VMEM((tm, V7X_LANES), F32),
            pltpu.SemaphoreType.DMA(()),
            pltpu.SemaphoreType.DMA(()),
        ],
        compiler_params=_params(("arbitrary", "arbitrary")),
        name="out_projection",
    )(att, rec, g_att, g_rec, w, res)


def _ffn_up_kernel(h_hbm, halo_ref, g_ref, wg_ref, wu_ref, cg_ref, cu_ref, bg_ref, bu_ref,
                   o_ref, h_ref, a_ref, stat_ref, sem, *, tm, seq):
    def norms():
        _norm_into(a_ref, FFN_HALO, 0, h_ref, g_ref, tm, stat_ref)
        seq_start = (pl.program_id(0) * tm) % seq == 0
        halo = _rms_rows(halo_ref[...], g_ref[...])
        a_ref[pl.ds(0, FFN_HALO), :] = jnp.where(seq_start, 0.0, halo).astype(a_ref.dtype)

    _row_tile_prologue([(h_hbm, h_ref, sem)], norms)
    a = a_ref[...]

    def conv(w_ref, c_ref, b_ref):
        y = jnp.dot(a, w_ref[...], preferred_element_type=F32)
        out = b_ref[...] + c_ref[pl.ds(FF_CONV - 1, 1), :] * y[FFN_HALO:, :]
        for k in range(FF_CONV - 1):
            shift = FF_CONV - 1 - k
            out = out + c_ref[pl.ds(k, 1), :] * y[FFN_HALO - shift:FFN_HALO - shift + tm, :]
        return out

    gate = conv(wg_ref, cg_ref, bg_ref)
    up = conv(wu_ref, cu_ref, bu_ref)
    o_ref[...] = (jax.nn.gelu(gate) * up).astype(o_ref.dtype)


def _ffn_up(h, g, w_up, w_conv, b_conv, seq, *, tm, tf):
    m, d = h.shape
    f = w_up.shape[1] // 2
    nf = f // tf
    kernel = functools.partial(_ffn_up_kernel, tm=tm, seq=seq)
    halo_blocks = tm // FFN_HALO
    b_conv = b_conv.reshape(1, 2 * f)
    return pl.pallas_call(
        kernel,
        grid=(m // tm, nf),
        in_specs=[
            pl.BlockSpec(memory_space=pl.ANY),
            pl.BlockSpec((FFN_HALO, d), lambda i, j: (jnp.maximum(i * halo_blocks - 1, 0), 0)),
            pl.BlockSpec((1, d), lambda i, j: (0, 0)),
            pl.BlockSpec((d, tf), lambda i, j: (0, j)),
            pl.BlockSpec((d, tf), lambda i, j: (0, nf + j)),
            pl.BlockSpec((FF_CONV, tf), lambda i, j: (0, j)),
            pl.BlockSpec((FF_CONV, tf), lambda i, j: (0, nf + j)),
            pl.BlockSpec((1, tf), lambda i, j: (0, j)),
            pl.BlockSpec((1, tf), lambda i, j: (0, nf + j)),
        ],
        out_specs=pl.BlockSpec((tm, tf), lambda i, j: (i, j)),
        out_shape=jax.ShapeDtypeStruct((m, f), BF16),
        scratch_shapes=[
            pltpu.VMEM((tm, d), F32),
            pltpu.VMEM((FFN_HALO + tm, d), BF16),
            pltpu.VMEM((tm, V7X_LANES), F32),
            pltpu.SemaphoreType.DMA(()),
        ],
        compiler_params=_params(("arbitrary", "arbitrary")),
        name="ffn_up_conv_gate",
    )(h, h, g, w_up, w_up, w_conv, w_conv, b_conv, b_conv)


def _matmul_res_kernel(a_ref, w_ref, res_ref, o_ref):
    @pl.when(pl.program_id(2) == 0)
    def _():
        o_ref[...] = res_ref[...]

    o_ref[...] += jnp.dot(a_ref[...], w_ref[...], preferred_element_type=F32)


def _matmul_res(a, w, res, *, tm, tn, tk):
    m, kdim = a.shape
    n = w.shape[1]
    return pl.pallas_call(
        _matmul_res_kernel,
        grid=(m // tm, n // tn, kdim // tk),
        in_specs=[
            pl.BlockSpec((tm, tk), lambda i, j, k: (i, k)),
            pl.BlockSpec((tk, tn), lambda i, j, k: (k, j)),
            pl.BlockSpec((tm, tn), lambda i, j, k: (i, j)),
        ],
        out_specs=pl.BlockSpec((tm, tn), lambda i, j, k: (i, j)),
        out_shape=jax.ShapeDtypeStruct((m, n), F32),
        compiler_params=_params(("parallel", "parallel", "arbitrary")),
        name="ffn_down_projection",
    )(a, w, res)


def _ple_kernel(h_hbm, g_ref, wg_ref, p_ref, wp_ref, res_ref, o_ref, h_ref, a_ref, stat_ref, sem):
    _row_tile_prologue(
        [(h_hbm, h_ref, sem)],
        lambda: _norm_into(a_ref, 0, 0, h_ref, g_ref, h_ref.shape[0], stat_ref))
    gate = _sigmoid(jnp.dot(a_ref[...], wg_ref[...], preferred_element_type=F32))
    emb = jnp.dot(p_ref[...].astype(BF16), wp_ref[...], preferred_element_type=F32)
    o_ref[...] = res_ref[...] + emb * gate


def _ple(h, g, w_gate, p, w_ple, *, tm, tn):
    m, d = h.shape
    n = w_gate.shape[1]
    pd = p.shape[1]
    return pl.pallas_call(
        _ple_kernel,
        grid=(m // tm, n // tn),
        in_specs=[
            pl.BlockSpec(memory_space=pl.ANY),
            pl.BlockSpec((1, d), lambda i, j: (0, 0)),
            pl.BlockSpec((d, tn), lambda i, j: (0, j)),
            pl.BlockSpec((tm, pd), lambda i, j: (i, 0)),
            pl.BlockSpec((pd, tn), lambda i, j: (0, j)),
            pl.BlockSpec((tm, tn), lambda i, j: (i, j)),
        ],
        out_specs=pl.BlockSpec((tm, tn), lambda i, j: (i, j)),
        out_shape=jax.ShapeDtypeStruct((m, n), F32),
        scratch_shapes=[
            pltpu.VMEM((tm, d), F32),
            pltpu.VMEM((tm, d), BF16),
            pltpu.VMEM((tm, V7X_LANES), F32),
            pltpu.SemaphoreType.DMA(()),
        ],
        compiler_params=_params(("arbitrary", "arbitrary")),
        name="ple_gate",
    )(h, g, w_gate, p, w_ple, h)


def _rmsnorm_kernel(x_ref, g_ref, o_ref):
    o_ref[...] = _rms_rows(x_ref[...], g_ref[...])


def _rmsnorm(x, g, *, tm):
    m, d = x.shape
    return pl.pallas_call(
        _rmsnorm_kernel,
        grid=(m // tm,),
        in_specs=[pl.BlockSpec((tm, d), lambda i: (i, 0)), pl.BlockSpec((1, d), lambda i: (0, 0))],
        out_specs=pl.BlockSpec((tm, d), lambda i: (i, 0)),
        out_shape=jax.ShapeDtypeStruct((m, d), F32),
        compiler_params=_params(("parallel",)),
        name="final_rmsnorm",
    )(x, g)


def kernel(x, p, g_mix, w_in, w_rconv, b_rconv, w_rg_a, b_rg_a, w_rg_x, b_rg_x, lam, g_att_out, g_rec_out, w_out, g_ffn, w_up, w_ffconv, b_ffconv, w_down, g_ple, w_ple, w_ple_gate, g_final):
    batch, seq, d_model = x.shape
    depth = w_in.shape[0]
    lru_width = lam.shape[-1]
    att_width = w_out.shape[1] - lru_width
    n_heads = att_width // HEAD_DIM
    m = batch * seq

    h = x.reshape(m, d_model)
    for l in range(depth):
        w_in_l = w_in[l].astype(BF16)
        qkv_scale = jnp.concatenate(
            [jnp.full((1, att_width), 1.0 / math.sqrt(HEAD_DIM), F32),
             jnp.ones((1, 2 * att_width), F32)], axis=1)
        g_mix_l = g_mix[l].reshape(1, d_model)
        qkv = _norm_matmul(h, g_mix_l, w_in_l, 0, 3 * att_width, qkv_scale, BF16,
                           tm=1024, tn=1024, name="in_projection_qkv")
        xy = _norm_matmul(h, g_mix_l, w_in_l, 3 * att_width, 2 * lru_width,
                          jnp.ones((1, 2 * lru_width), F32), F32,
                          tm=1024, tn=1024, name="in_projection_lru")
        att, (w_out_l, w_up_l, w_down_l, w_gate_l) = _attention(
            qkv, [w_out[l], w_up[l], w_down[l], w_ple_gate[l]], batch, seq, n_heads,
            tq=256, heads=4)
        rec = _rglru(xy, w_rconv[l], b_rconv[l], w_rg_a[l], b_rg_a[l], w_rg_x[l], b_rg_x[l],
                     lam[l], batch, seq, ts=1024, tw=1024)
        h = _outproj(att, rec, g_att_out[l].reshape(1, att_width),
                     g_rec_out[l].reshape(1, lru_width), w_out_l, h, tm=1024, tn=512)
        act = _ffn_up(h, g_ffn[l].reshape(1, d_model), w_up_l, w_ffconv[l],
                      b_ffconv[l], seq, tm=1024, tf=512)
        h = _matmul_res(act, w_down_l, h, tm=1024, tn=1024, tk=4096)
        h = _ple(h, g_ple[l].reshape(1, d_model), w_gate_l,
                 p[l].reshape(m, -1), w_ple[l].astype(BF16), tm=1024, tn=512)
    out = _rmsnorm(h, g_final.reshape(1, d_model), tm=256)
    return out.reshape(batch, seq, d_model)
```

```python
import functools
import math

import jax
import jax.numpy as jnp
from jax import lax
from jax.experimental import pallas as pl
from jax.experimental.pallas import tpu as pltpu

F32 = jnp.float32
BF16 = jnp.bfloat16

EPS = 1e-6
HEAD_DIM = 128
N_LRU_BLOCKS = 16
RG_C = 8.0
REC_CONV = 4
FF_CONV = 3

V7X_LANES = 128
V7X_SUBLANES = 8
V7X_VMEM_LIMIT_BYTES = 60000 * 1024

V7X_BF16_ROWS = 2 * V7X_SUBLANES

HALO = V7X_SUBLANES
FFN_HALO = V7X_BF16_ROWS
FFN_CHUNK_ROWS = 128
NORM_ROWS = 32


def _params(semantics, vmem_bytes=V7X_VMEM_LIMIT_BYTES):
    return pltpu.CompilerParams(dimension_semantics=semantics, vmem_limit_bytes=vmem_bytes)


def _rms_rows(x, g):
    ms = jnp.mean(x * x, axis=-1, keepdims=True)
    return x * lax.rsqrt(ms + EPS) * g


def _norm_into(dst_ref, dst_row0, dst_col0, src_ref, g_ref, rows, stat_ref):
    width = src_ref.shape[-1]
    lane_tiles = width // V7X_LANES
    chunk = min(NORM_ROWS, rows)

    def sum_squares(c, carry):
        r0 = pl.multiple_of(c * chunk, chunk)
        x = src_ref[pl.ds(r0, chunk), :]
        sq = x * x
        parts = [sq[:, k * V7X_LANES:(k + 1) * V7X_LANES] for k in range(lane_tiles)]
        while len(parts) > 1:
            parts = [a + b for a, b in zip(parts[0::2], parts[1::2])] + parts[len(parts) & ~1:]
        stat_ref[pl.ds(r0, chunk), :] = parts[0]
        return carry

    lax.fori_loop(0, rows // chunk, sum_squares, 0)
    ms = jnp.sum(stat_ref[pl.ds(0, rows), :], axis=-1, keepdims=True) * (1.0 / width)
    stat_ref[pl.ds(0, rows), :] = jnp.broadcast_to(lax.rsqrt(ms + EPS), (rows, V7X_LANES))

    def scale(c, carry):
        r0 = pl.multiple_of(c * chunk, chunk)
        rstd = jnp.concatenate([stat_ref[pl.ds(r0, chunk), :]] * lane_tiles, axis=1)
        y = src_ref[pl.ds(r0, chunk), :] * rstd * g_ref[...]
        dst_ref[pl.ds(dst_row0 + r0, chunk), pl.ds(dst_col0, width)] = y.astype(dst_ref.dtype)
        return carry

    lax.fori_loop(0, rows // chunk, scale, 0)


def _row_tile_prologue(sources, consume):
    i = pl.program_id(0)

    def copies(tile):
        return [pltpu.make_async_copy(hbm.at[pl.ds(tile * buf.shape[0], buf.shape[0]), :], buf, sem)
                for hbm, buf, sem in sources]

    @pl.when(pl.program_id(1) == 0)
    def _():
        @pl.when(i == 0)
        def _():
            for c in copies(0):
                c.start()

        for c in copies(i):
            c.wait()
        consume()

        @pl.when(i + 1 < pl.num_programs(0))
        def _():
            for c in copies(i + 1):
                c.start()


def _norm_matmul_kernel(x_hbm, g_ref, w_ref, cs_ref, o_ref, x_ref, a_ref, stat_ref, sem):
    _row_tile_prologue(
        [(x_hbm, x_ref, sem)],
        lambda: _norm_into(a_ref, 0, 0, x_ref, g_ref, x_ref.shape[0], stat_ref))
    y = jnp.dot(a_ref[...], w_ref[...], preferred_element_type=F32)
    o_ref[...] = (y * cs_ref[...]).astype(o_ref.dtype)


def _norm_matmul(x, g, w, col0, n, col_scale, out_dtype, *, tm, tn, name):
    m, d = x.shape
    j0 = col0 // tn
    return pl.pallas_call(
        _norm_matmul_kernel,
        grid=(m // tm, n // tn),
        in_specs=[
            pl.BlockSpec(memory_space=pl.ANY),
            pl.BlockSpec((1, d), lambda i, j: (0, 0)),
            pl.BlockSpec((d, tn), lambda i, j: (0, j0 + j)),
            pl.BlockSpec((1, tn), lambda i, j: (0, j)),
        ],
        out_specs=pl.BlockSpec((tm, tn), lambda i, j: (i, j)),
        out_shape=jax.ShapeDtypeStruct((m, n), out_dtype),
        scratch_shapes=[
            pltpu.VMEM((tm, d), F32),
            pltpu.VMEM((tm, d), BF16),
            pltpu.VMEM((tm, V7X_LANES), F32),
            pltpu.SemaphoreType.DMA(()),
        ],
        compiler_params=_params(("arbitrary", "arbitrary")),
        name=name,
    )(x, g, w, col_scale)


MASKED_SCORE = -1e30


def _attn_scores(q, k, diagonal):
    tq, tk = q.shape[0], k.shape[0]
    z = lax.dot_general(q, k, (((1,), (1,)), ((), ())), preferred_element_type=F32)
    sp = jnp.maximum(z, 0.0) + jnp.log(1.0 + jnp.exp(-jnp.abs(z)))
    if diagonal:
        row = lax.broadcasted_iota(jnp.int32, (tq, tk), 0)
        col = lax.broadcasted_iota(jnp.int32, (tq, tk), 1)
        mask = col < row
        sp = jnp.where(mask, sp, 0.0)
        z = jnp.where(mask, z, MASKED_SCORE)
    return z, sp.astype(BF16)


def _attn_accumulate(z, s, v, acc_ref, csum_ref, cols):
    tk = z.shape[1]
    csum = csum_ref[:, cols]
    a = jnp.exp(z + s + jnp.concatenate([csum] * (tk // V7X_LANES), axis=1))
    acc_ref[:, cols] += jnp.dot(a.astype(BF16), v, preferred_element_type=F32)
    csum_ref[:, cols] = csum + jnp.broadcast_to(s[:, :1], csum.shape)


def _attn_kernel(*refs, tq, tk, heads, n_casts):
    q_ref, k_ref, v_ref, negu_ref = refs[:4]
    cast_in = refs[4:4 + n_casts]
    o_ref = refs[4 + n_casts]
    cast_out = refs[5 + n_casts:5 + 2 * n_casts]
    acc_ref, csum_ref, z0_ref, sp0_ref, z1_ref, sp1_ref = refs[5 + 2 * n_casts:]
    qi = pl.program_id(2)
    acc_ref[...] = jnp.zeros_like(acc_ref)
    csum_ref[...] = jnp.zeros_like(csum_ref)

    def convert_slabs():
        for src, dst in zip(cast_in, cast_out):
            dst[...] = src[...].astype(dst.dtype)

    def key_rows(n):
        return pl.ds(pl.multiple_of((qi - n) * tk, tk), tk)

    def scores(n, z_ref, sp_ref, diagonal=False):
        rows = key_rows(n)
        for g in range(heads):
            cols = slice(g * HEAD_DIM, (g + 1) * HEAD_DIM)
            z, sp = _attn_scores(q_ref[:, cols], k_ref[rows, cols], diagonal)
            z_ref[g * tq:(g + 1) * tq, :] = z
            sp_ref[g * tq:(g + 1) * tq, :] = sp

    def accumulate(n, z_ref, sp_ref):
        rows = key_rows(n)
        s_all = jnp.dot(sp_ref[...], negu_ref[...], preferred_element_type=F32)
        for g in range(heads):
            cols = slice(g * HEAD_DIM, (g + 1) * HEAD_DIM)
            tile = slice(g * tq, (g + 1) * tq)
            _attn_accumulate(z_ref[tile, :], s_all[tile, :], v_ref[rows, cols],
                             acc_ref, csum_ref, cols)

    scores(0, z0_ref, sp0_ref, diagonal=True)

    def pair(p, carry):
        n = 2 * p
        accumulate(n, z0_ref, sp0_ref)
        scores(n + 1, z1_ref, sp1_ref)
        accumulate(n + 1, z1_ref, sp1_ref)
        scores(n + 2, z0_ref, sp0_ref)
        return carry

    lax.fori_loop(0, qi // 2, pair, 0)
    n_done = 2 * (qi // 2)

    @pl.when(qi % 2 == 1)
    def _():
        accumulate(n_done, z0_ref, sp0_ref)
        scores(n_done + 1, z1_ref, sp1_ref)
        accumulate(n_done + 1, z1_ref, sp1_ref)
        convert_slabs()

    @pl.when(qi % 2 == 0)
    def _():
        accumulate(n_done, z0_ref, sp0_ref)
        convert_slabs()

    o_ref[...] = acc_ref[...].astype(o_ref.dtype)


def _attention(qkv, casts, batch, seq, n_heads, *, tq, heads):
    tk = tq
    nq = seq // tq
    width = heads * HEAD_DIM
    groups = n_heads // heads
    steps = batch * groups * nq
    row = lax.broadcasted_iota(jnp.int32, (tk, tk), 0)
    col = lax.broadcasted_iota(jnp.int32, (tk, tk), 1)
    negu = -(row >= col).astype(BF16)
    kernel = functools.partial(_attn_kernel, tq=tq, tk=tk, heads=heads, n_casts=len(casts))
    for w in casts:
        assert w.shape[0] % (steps * V7X_BF16_ROWS) == 0, w.shape
    slab_specs = [pl.BlockSpec((w.shape[0] // steps, w.shape[1]),
                               lambda b, h, i: ((b * groups + h) * nq + i, 0)) for w in casts]
    outs = pl.pallas_call(
        kernel,
        grid=(batch, groups, nq),
        in_specs=[
            pl.BlockSpec((tq, width), lambda b, h, i: (b * nq + i, h)),
            pl.BlockSpec((seq, width), lambda b, h, i: (b, groups + h)),
            pl.BlockSpec((seq, width), lambda b, h, i: (b, 2 * groups + h)),
            pl.BlockSpec((tk, tk), lambda b, h, i: (0, 0)),
        ] + slab_specs,
        out_specs=[pl.BlockSpec((tq, width), lambda b, h, i: (b * nq + i, h))] + slab_specs,
        out_shape=[jax.ShapeDtypeStruct((batch * seq, n_heads * HEAD_DIM), F32)]
        + [jax.ShapeDtypeStruct(w.shape, BF16) for w in casts],
        scratch_shapes=[
            pltpu.VMEM((tq, width), F32),
            pltpu.VMEM((tq, width), F32),
            pltpu.VMEM((heads * tq, tk), F32),
            pltpu.VMEM((heads * tq, tk), BF16),
            pltpu.VMEM((heads * tq, tk), F32),
            pltpu.VMEM((heads * tq, tk), BF16),
        ],
        compiler_params=_params(("parallel", "parallel", "arbitrary")),
        name="stickbreak_attention",
    )(qkv, qkv, qkv, negu, *casts)
    return outs[0], outs[1:]


RSQRT_FLOOR = 1e-30
SCAN_UNROLL = 4


def _softplus(x):
    return jnp.maximum(x, 0.0) + jnp.log1p(jnp.exp(-jnp.abs(x)))


def _sigmoid(x):
    return 0.5 * (jnp.tanh(0.5 * x) + 1.0)


def _rglru_kernel(xr_ref, yr_ref, wc_ref, bc_ref, wa_ref, ba_ref, wx_ref, bx_ref, lam_ref,
                  o_ref, xext_ref, a_ref, b_ref, h_ref, *, ts, tw):
    si = pl.program_id(2)

    @pl.when(si == 0)
    def _():
        xext_ref[pl.ds(0, HALO), :] = jnp.zeros((HALO, tw), F32)
        h_ref[...] = jnp.zeros_like(h_ref)

    @pl.when(si != 0)
    def _():
        xext_ref[pl.ds(0, HALO), :] = xext_ref[pl.ds(ts, HALO), :]

    xext_ref[pl.ds(HALO, ts), :] = xr_ref[...]

    xc = bc_ref[...] + wc_ref[pl.ds(REC_CONV - 1, 1), :] * xr_ref[...]
    for k in range(REC_CONV - 1):
        shift = REC_CONV - 1 - k
        xc = xc + wc_ref[pl.ds(k, 1), :] * xext_ref[pl.ds(HALO - shift, ts), :]

    neg_c_sp = -RG_C * _softplus(-lam_ref[...])
    xc16 = xc.astype(BF16)
    for n in range(tw // V7X_LANES):
        cols = slice(n * V7X_LANES, (n + 1) * V7X_LANES)
        xb = xc16[:, cols]
        r = _sigmoid(jnp.dot(xb, wa_ref[n], preferred_element_type=F32) + ba_ref[:, cols])
        i = _sigmoid(jnp.dot(xb, wx_ref[n], preferred_element_type=F32) + bx_ref[:, cols])
        log_a = neg_c_sp[:, cols] * r
        a = jnp.exp(log_a)
        u = (1.0 - a) * (1.0 + a)
        mult = u * lax.rsqrt(jnp.maximum(u, RSQRT_FLOOR))
        a_ref[:, cols] = a
        b_ref[:, cols] = mult * (i * xc[:, cols])

    row = lax.broadcasted_iota(jnp.int32, (V7X_SUBLANES, tw), 0)

    def group(gi, h_prev):
        r0 = pl.multiple_of(gi * V7X_SUBLANES, V7X_SUBLANES)
        a = a_ref[pl.ds(r0, V7X_SUBLANES), :]
        b = b_ref[pl.ds(r0, V7X_SUBLANES), :]
        for d in (1, 2, 4):
            keep = row >= d
            a_sh = jnp.where(keep, pltpu.roll(a, d, 0), 1.0)
            b_sh = jnp.where(keep, pltpu.roll(b, d, 0), 0.0)
            b = a * b_sh + b
            a = a * a_sh
        h = a * h_prev + b
        b_ref[pl.ds(r0, V7X_SUBLANES), :] = h
        return jnp.broadcast_to(h[V7X_SUBLANES - 1:, :], (V7X_SUBLANES, tw))

    h_last = lax.fori_loop(0, ts // V7X_SUBLANES, group, h_ref[...], unroll=SCAN_UNROLL)
    h_ref[...] = h_last
    o_ref[...] = jax.nn.gelu(yr_ref[...]) * b_ref[...]


def _rglru(xy, w_rconv, b_rconv, w_rg_a, b_rg_a, w_rg_x, b_rg_x, lam, batch, seq, *, ts, tw):
    width = lam.shape[-1]
    nw = width // tw
    ns = seq // ts
    gb = tw // V7X_LANES
    row_spec = pl.BlockSpec((1, tw), lambda b, w, s: (0, w))
    kernel = functools.partial(_rglru_kernel, ts=ts, tw=tw)
    return pl.pallas_call(
        kernel,
        grid=(batch, nw, ns),
        in_specs=[
            pl.BlockSpec((ts, tw), lambda b, w, s: (b * ns + s, w)),
            pl.BlockSpec((ts, tw), lambda b, w, s: (b * ns + s, nw + w)),
            pl.BlockSpec((REC_CONV, tw), lambda b, w, s: (0, w)),
            row_spec,
            pl.BlockSpec((gb, V7X_LANES, V7X_LANES), lambda b, w, s: (w, 0, 0)),
            row_spec,
            pl.BlockSpec((gb, V7X_LANES, V7X_LANES), lambda b, w, s: (w, 0, 0)),
            row_spec,
            row_spec,
        ],
        out_specs=pl.BlockSpec((ts, tw), lambda b, w, s: (b * ns + s, w)),
        out_shape=jax.ShapeDtypeStruct((batch * seq, width), F32),
        scratch_shapes=[
            pltpu.VMEM((ts + HALO, tw), F32),
            pltpu.VMEM((ts, tw), F32),
            pltpu.VMEM((ts, tw), F32),
            pltpu.VMEM((V7X_SUBLANES, tw), F32),
        ],
        compiler_params=_params(("parallel", "parallel", "arbitrary")),
        name="rglru_branch",
    )(xy, xy, w_rconv, b_rconv.reshape(1, width), w_rg_a.astype(BF16), b_rg_a.reshape(1, width),
      w_rg_x.astype(BF16), b_rg_x.reshape(1, width), lam.reshape(1, width))


def _outproj_kernel(att_hbm, rec_hbm, ga_ref, gr_ref, w_ref, res_ref, o_ref,
                    att_ref, rec_ref, a_ref, stat_ref, att_sem, rec_sem):
    def norms():
        _norm_into(a_ref, 0, 0, att_ref, ga_ref, att_ref.shape[0], stat_ref)
        _norm_into(a_ref, 0, att_ref.shape[1], rec_ref, gr_ref, rec_ref.shape[0], stat_ref)

    _row_tile_prologue([(att_hbm, att_ref, att_sem), (rec_hbm, rec_ref, rec_sem)], norms)
    o_ref[...] = res_ref[...] + jnp.dot(a_ref[...], w_ref[...], preferred_element_type=F32)


def _outproj(att, rec, g_att, g_rec, w, res, *, tm, tn):
    m, wa = att.shape
    wr = rec.shape[1]
    n = w.shape[1]
    return pl.pallas_call(
        _outproj_kernel,
        grid=(m // tm, n // tn),
        in_specs=[
            pl.BlockSpec(memory_space=pl.ANY),
            pl.BlockSpec(memory_space=pl.ANY),
            pl.BlockSpec((1, wa), lambda i, j: (0, 0)),
            pl.BlockSpec((1, wr), lambda i, j: (0, 0)),
            pl.BlockSpec((wa + wr, tn), lambda i, j: (0, j)),
            pl.BlockSpec((tm, tn), lambda i, j: (i, j)),
        ],
        out_specs=pl.BlockSpec((tm, tn), lambda i, j: (i, j)),
        out_shape=jax.ShapeDtypeStruct((m, n), F32),
        scratch_shapes=[
            pltpu.VMEM((tm, wa), F32),
            pltpu.VMEM((tm, wr), F32),
            pltpu.VMEM((tm, wa + wr), BF16),
            pltpu.VMEM((tm, V7X_LANES), F32),
            pltpu.SemaphoreType.DMA(()),
            pltpu.SemaphoreType.DMA(()),
        ],
        compiler_params=_params(("arbitrary", "arbitrary")),
        name="out_projection",
    )(att, rec, g_att, g_rec, w, res)


def _chunked_row_tile_prologue(hbm, buf, sems, consume_chunk, *, tm):
    i = pl.program_id(0)
    rc = buf.shape[1]
    n_chunks = tm // rc

    def copy(tile, c):
        row0 = pl.multiple_of(tile * tm + c * rc, rc)
        return pltpu.make_async_copy(hbm.at[pl.ds(row0, rc), :], buf.at[c % 2], sems.at[c % 2])

    @pl.when(pl.program_id(1) == 0)
    def _():
        @pl.when(i == 0)
        def _():
            copy(0, 0).start()
            copy(0, 1).start()

        def body(c, carry):
            copy(i, c).wait()
            consume_chunk(pl.multiple_of(c * rc, rc), buf[c % 2])

            @pl.when(c + 2 < n_chunks)
            def _():
                copy(i, c + 2).start()

            return carry

        lax.fori_loop(0, n_chunks, body, 0)

        @pl.when(i + 1 < pl.num_programs(0))
        def _():
            copy(i + 1, 0).start()
            copy(i + 1, 1).start()


def _ffn_up_kernel(h_hbm, halo_ref, g_ref, wg_ref, wu_ref, cg_ref, cu_ref, bg_ref, bu_ref,
                   o_ref, h_ref, a_ref, sems, *, tm, seq):
    def norm_chunk(row0, rows):
        y = _rms_rows(rows, g_ref[...])
        a_ref[pl.ds(FFN_HALO + row0, rows.shape[0]), :] = y.astype(a_ref.dtype)

    _chunked_row_tile_prologue(h_hbm, h_ref, sems, norm_chunk, tm=tm)

    @pl.when(pl.program_id(1) == 0)
    def _():
        seq_start = (pl.program_id(0) * tm) % seq == 0
        halo = _rms_rows(halo_ref[...], g_ref[...])
        a_ref[pl.ds(0, FFN_HALO), :] = jnp.where(seq_start, 0.0, halo).astype(a_ref.dtype)

    a = a_ref[...]

    def conv(w_ref, c_ref, b_ref):
        y = jnp.dot(a, w_ref[...], preferred_element_type=F32)
        out = b_ref[...] + c_ref[pl.ds(FF_CONV - 1, 1), :] * y[FFN_HALO:, :]
        for k in range(FF_CONV - 1):
            shift = FF_CONV - 1 - k
            out = out + c_ref[pl.ds(k, 1), :] * y[FFN_HALO - shift:FFN_HALO - shift + tm, :]
        return out

    gate = conv(wg_ref, cg_ref, bg_ref)
    up = conv(wu_ref, cu_ref, bu_ref)
    o_ref[...] = (jax.nn.gelu(gate) * up).astype(o_ref.dtype)


def _ffn_up(h, g, w_up, w_conv, b_conv, seq, *, tm, tf):
    m, d = h.shape
    f = w_up.shape[1] // 2
    nf = f // tf
    kernel = functools.partial(_ffn_up_kernel, tm=tm, seq=seq)
    halo_blocks = tm // FFN_HALO
    b_conv = b_conv.reshape(1, 2 * f)
    return pl.pallas_call(
        kernel,
        grid=(m // tm, nf),
        in_specs=[
            pl.BlockSpec(memory_space=pl.ANY),
            pl.BlockSpec((FFN_HALO, d), lambda i, j: (jnp.maximum(i * halo_blocks - 1, 0), 0)),
            pl.BlockSpec((1, d), lambda i, j: (0, 0)),
            pl.BlockSpec((d, tf), lambda i, j: (0, j)),
            pl.BlockSpec((d, tf), lambda i, j: (0, nf + j)),
            pl.BlockSpec((FF_CONV, tf), lambda i, j: (0, j)),
            pl.BlockSpec((FF_CONV, tf), lambda i, j: (0, nf + j)),
            pl.BlockSpec((1, tf), lambda i, j: (0, j)),
            pl.BlockSpec((1, tf), lambda i, j: (0, nf + j)),
        ],
        out_specs=pl.BlockSpec((tm, tf), lambda i, j: (i, j)),
        out_shape=jax.ShapeDtypeStruct((m, f), BF16),
        scratch_shapes=[
            pltpu.VMEM((2, FFN_CHUNK_ROWS, d), F32),
            pltpu.VMEM((FFN_HALO + tm, d), BF16),
            pltpu.SemaphoreType.DMA((2,)),
        ],
        compiler_params=_params(("arbitrary", "arbitrary")),
        name="ffn_up_conv_gate",
    )(h, h, g, w_up, w_up, w_conv, w_conv, b_conv, b_conv)


def _matmul_res_kernel(a_ref, w_ref, res_ref, o_ref):
    @pl.when(pl.program_id(2) == 0)
    def _():
        o_ref[...] = res_ref[...]

    o_ref[...] += jnp.dot(a_ref[...], w_ref[...], preferred_element_type=F32)


def _matmul_res(a, w, res, *, tm, tn, tk):
    m, kdim = a.shape
    n = w.shape[1]
    return pl.pallas_call(
        _matmul_res_kernel,
        grid=(m // tm, n // tn, kdim // tk),
        in_specs=[
            pl.BlockSpec((tm, tk), lambda i, j, k: (i, k)),
            pl.BlockSpec((tk, tn), lambda i, j, k: (k, j)),
            pl.BlockSpec((tm, tn), lambda i, j, k: (i, j)),
        ],
        out_specs=pl.BlockSpec((tm, tn), lambda i, j, k: (i, j)),
        out_shape=jax.ShapeDtypeStruct((m, n), F32),
        compiler_params=_params(("parallel", "parallel", "arbitrary")),
        name="ffn_down_projection",
    )(a, w, res)


def _ple_kernel(h_hbm, g_ref, wg_ref, p_ref, wp_ref, res_ref, o_ref, h_ref, a_ref, stat_ref, sem):
    _row_tile_prologue(
        [(h_hbm, h_ref, sem)],
        lambda: _norm_into(a_ref, 0, 0, h_ref, g_ref, h_ref.shape[0], stat_ref))
    gate = _sigmoid(jnp.dot(a_ref[...], wg_ref[...], preferred_element_type=F32))
    emb = jnp.dot(p_ref[...].astype(BF16), wp_ref[...], preferred_element_type=F32)
    o_ref[...] = res_ref[...] + emb * gate


def _ple(h, g, w_gate, p, w_ple, *, tm, tn):
    m, d = h.shape
    n = w_gate.shape[1]
    pd = p.shape[1]
    return pl.pallas_call(
        _ple_kernel,
        grid=(m // tm, n // tn),
        in_specs=[
            pl.BlockSpec(memory_space=pl.ANY),
            pl.BlockSpec((1, d), lambda i, j: (0, 0)),
            pl.BlockSpec((d, tn), lambda i, j: (0, j)),
            pl.BlockSpec((tm, pd), lambda i, j: (i, 0)),
            pl.BlockSpec((pd, tn), lambda i, j: (0, j)),
            pl.BlockSpec((tm, tn), lambda i, j: (i, j)),
        ],
        out_specs=pl.BlockSpec((tm, tn), lambda i, j: (i, j)),
        out_shape=jax.ShapeDtypeStruct((m, n), F32),
        scratch_shapes=[
            pltpu.VMEM((tm, d), F32),
            pltpu.VMEM((tm, d), BF16),
            pltpu.VMEM((tm, V7X_LANES), F32),
            pltpu.SemaphoreType.DMA(()),
        ],
        compiler_params=_params(("arbitrary", "arbitrary")),
        name="ple_gate",
    )(h, g, w_gate, p, w_ple, h)


def _rmsnorm_kernel(x_ref, g_ref, o_ref):
    o_ref[...] = _rms_rows(x_ref[...], g_ref[...])


def _rmsnorm(x, g, *, tm):
    m, d = x.shape
    return pl.pallas_call(
        _rmsnorm_kernel,
        grid=(m // tm,),
        in_specs=[pl.BlockSpec((tm, d), lambda i: (i, 0)), pl.BlockSpec((1, d), lambda i: (0, 0))],
        out_specs=pl.BlockSpec((tm, d), lambda i: (i, 0)),
        out_shape=jax.ShapeDtypeStruct((m, d), F32),
        compiler_params=_params(("parallel",)),
        name="final_rmsnorm",
    )(x, g)


def kernel(x, p, g_mix, w_in, w_rconv, b_rconv, w_rg_a, b_rg_a, w_rg_x, b_rg_x, lam, g_att_out, g_rec_out, w_out, g_ffn, w_up, w_ffconv, b_ffconv, w_down, g_ple, w_ple, w_ple_gate, g_final):
    batch, seq, d_model = x.shape
    depth = w_in.shape[0]
    lru_width = lam.shape[-1]
    att_width = w_out.shape[1] - lru_width
    n_heads = att_width // HEAD_DIM
    m = batch * seq

    h = x.reshape(m, d_model)
    for l in range(depth):
        w_in_l = w_in[l].astype(BF16)
        qkv_scale = jnp.concatenate(
            [jnp.full((1, att_width), 1.0 / math.sqrt(HEAD_DIM), F32),
             jnp.ones((1, 2 * att_width), F32)], axis=1)
        g_mix_l = g_mix[l].reshape(1, d_model)
        qkv = _norm_matmul(h, g_mix_l, w_in_l, 0, 3 * att_width, qkv_scale, BF16,
                           tm=1024, tn=1024, name="in_projection_qkv")
        xy = _norm_matmul(h, g_mix_l, w_in_l, 3 * att_width, 2 * lru_width,
                          jnp.ones((1, 2 * lru_width), F32), F32,
                          tm=1024, tn=1024, name="in_projection_lru")
        att, (w_out_l, w_up_l, w_down_l, w_gate_l) = _attention(
            qkv, [w_out[l], w_up[l], w_down[l], w_ple_gate[l]], batch, seq, n_heads,
            tq=256, heads=4)
        rec = _rglru(xy, w_rconv[l], b_rconv[l], w_rg_a[l], b_rg_a[l], w_rg_x[l], b_rg_x[l],
                     lam[l], batch, seq, ts=1024, tw=1024)
        h = _outproj(att, rec, g_att_out[l].reshape(1, att_width),
                     g_rec_out[l].reshape(1, lru_width), w_out_l, h, tm=1024, tn=512)
        act = _ffn_up(h, g_ffn[l].reshape(1, d_model), w_up_l, w_ffconv[l],
                      b_ffconv[l], seq, tm=1024, tf=768)
        h = _matmul_res(act, w_down_l, h, tm=1024, tn=1024, tk=4096)
        h = _ple(h, g_ple[l].reshape(1, d_model), w_gate_l,
                 p[l].reshape(m, -1), w_ple[l].astype(BF16), tm=1024, tn=512)
    out = _rmsnorm(h, g_final.reshape(1, d_model), tm=512)
    return out.reshape(batch, seq, d_model)
```

```python
import functools
import math

import jax
import jax.numpy as jnp
from jax import lax
from jax.experimental import pallas as pl
from jax.experimental.pallas import tpu as pltpu

F32 = jnp.float32
BF16 = jnp.bfloat16

EPS = 1e-6
HEAD_DIM = 128
N_LRU_BLOCKS = 16
RG_C = 8.0
REC_CONV = 4
FF_CONV = 3

V7X_LANES = 128
V7X_SUBLANES = 8
V7X_VMEM_LIMIT_BYTES = 60000 * 1024

V7X_BF16_ROWS = 2 * V7X_SUBLANES

HALO = V7X_SUBLANES
FFN_HALO = V7X_BF16_ROWS
NORM_ROWS = 32

ROW_TILE = 1024
IN_PROJ_COLS = 1024
OUT_PROJ_COLS = 512
FFN_HIDDEN_COLS = 512
FFN_DOWN_COLS = 1024
FFN_DOWN_DEPTH = 4096
PLE_COLS = 512
ATTN_BLOCK = 256
ATTN_HEADS = 4
LRU_TIME_TILE = 1024
LRU_WIDTH_TILE = 1024
FINAL_NORM_ROWS = 512


def _params(semantics, vmem_bytes=V7X_VMEM_LIMIT_BYTES):
    return pltpu.CompilerParams(dimension_semantics=semantics, vmem_limit_bytes=vmem_bytes)


def _rms_rows(x, g):
    ms = jnp.mean(x * x, axis=-1, keepdims=True)
    return x * lax.rsqrt(ms + EPS) * g


def _norm_into(dst_ref, dst_row0, dst_col0, src_ref, g_ref, rows, stat_ref):
    width = src_ref.shape[-1]
    lane_tiles = width // V7X_LANES
    chunk = min(NORM_ROWS, rows)

    def sum_squares(c, carry):
        r0 = pl.multiple_of(c * chunk, chunk)
        x = src_ref[pl.ds(r0, chunk), :]
        sq = x * x
        parts = [sq[:, k * V7X_LANES:(k + 1) * V7X_LANES] for k in range(lane_tiles)]
        while len(parts) > 1:
            parts = [a + b for a, b in zip(parts[0::2], parts[1::2])] + parts[len(parts) & ~1:]
        stat_ref[pl.ds(r0, chunk), :] = parts[0]
        return carry

    lax.fori_loop(0, rows // chunk, sum_squares, 0)
    ms = jnp.sum(stat_ref[pl.ds(0, rows), :], axis=-1, keepdims=True) * (1.0 / width)
    stat_ref[pl.ds(0, rows), :] = jnp.broadcast_to(lax.rsqrt(ms + EPS), (rows, V7X_LANES))

    def scale(c, carry):
        r0 = pl.multiple_of(c * chunk, chunk)
        rstd = jnp.concatenate([stat_ref[pl.ds(r0, chunk), :]] * lane_tiles, axis=1)
        y = src_ref[pl.ds(r0, chunk), :] * rstd * g_ref[...]
        dst_ref[pl.ds(dst_row0 + r0, chunk), pl.ds(dst_col0, width)] = y.astype(dst_ref.dtype)
        return carry

    lax.fori_loop(0, rows // chunk, scale, 0)


def _row_tile_prologue(sources, consume):
    i = pl.program_id(0)

    def copies(tile):
        return [pltpu.make_async_copy(hbm.at[pl.ds(tile * buf.shape[0], buf.shape[0]), :], buf, sem)
                for hbm, buf, sem in sources]

    @pl.when(pl.program_id(1) == 0)
    def _():
        @pl.when(i == 0)
        def _():
            for c in copies(0):
                c.start()

        for c in copies(i):
            c.wait()
        consume()

        @pl.when(i + 1 < pl.num_programs(0))
        def _():
            for c in copies(i + 1):
                c.start()


def _norm_matmul_kernel(x_hbm, g_ref, w_ref, cs_ref, o_ref, x_ref, a_ref, stat_ref, sem):
    _row_tile_prologue(
        [(x_hbm, x_ref, sem)],
        lambda: _norm_into(a_ref, 0, 0, x_ref, g_ref, x_ref.shape[0], stat_ref))
    y = jnp.dot(a_ref[...], w_ref[...], preferred_element_type=F32)
    o_ref[...] = (y * cs_ref[...]).astype(o_ref.dtype)


def _norm_matmul(x, g, w, col0, n, col_scale, out_dtype, *, tm, tn, name):
    m, d = x.shape
    j0 = col0 // tn
    return pl.pallas_call(
        _norm_matmul_kernel,
        grid=(m // tm, n // tn),
        in_specs=[
            pl.BlockSpec(memory_space=pl.ANY),
            pl.BlockSpec((1, d), lambda i, j: (0, 0)),
            pl.BlockSpec((d, tn), lambda i, j: (0, j0 + j)),
            pl.BlockSpec((1, tn), lambda i, j: (0, j)),
        ],
        out_specs=pl.BlockSpec((tm, tn), lambda i, j: (i, j)),
        out_shape=jax.ShapeDtypeStruct((m, n), out_dtype),
        scratch_shapes=[
            pltpu.VMEM((tm, d), F32),
            pltpu.VMEM((tm, d), BF16),
            pltpu.VMEM((tm, V7X_LANES), F32),
            pltpu.SemaphoreType.DMA(()),
        ],
        compiler_params=_params(("arbitrary", "arbitrary")),
        name=name,
    )(x, g, w, col_scale)


MASKED_SCORE = -1e30


def _attn_scores(q, k, diagonal):
    tq, tk = q.shape[0], k.shape[0]
    z = lax.dot_general(q, k, (((1,), (1,)), ((), ())), preferred_element_type=F32)
    sp = jnp.maximum(z, 0.0) + jnp.log(1.0 + jnp.exp(-jnp.abs(z)))
    if diagonal:
        row = lax.broadcasted_iota(jnp.int32, (tq, tk), 0)
        col = lax.broadcasted_iota(jnp.int32, (tq, tk), 1)
        mask = col < row
        sp = jnp.where(mask, sp, 0.0)
        z = jnp.where(mask, z, MASKED_SCORE)
    return z, sp.astype(BF16)


def _attn_accumulate(z, s, v, acc_ref, csum_ref, cols):
    tk = z.shape[1]
    csum = csum_ref[:, cols]
    a = jnp.exp(z + s + jnp.concatenate([csum] * (tk // V7X_LANES), axis=1))
    acc_ref[:, cols] += jnp.dot(a.astype(BF16), v, preferred_element_type=F32)
    csum_ref[:, cols] = csum + jnp.broadcast_to(s[:, :1], csum.shape)


def _attn_kernel(*refs, tq, tk, heads, n_casts):
    q_ref, k_ref, v_ref, negu_ref = refs[:4]
    cast_in = refs[4:4 + n_casts]
    o_ref = refs[4 + n_casts]
    cast_out = refs[5 + n_casts:5 + 2 * n_casts]
    acc_ref, csum_ref, z0_ref, sp0_ref, z1_ref, sp1_ref = refs[5 + 2 * n_casts:]
    qi = pl.program_id(2)
    acc_ref[...] = jnp.zeros_like(acc_ref)
    csum_ref[...] = jnp.zeros_like(csum_ref)

    def convert_slabs():
        for src, dst in zip(cast_in, cast_out):
            dst[...] = src[...].astype(dst.dtype)

    def key_rows(n):
        return pl.ds(pl.multiple_of((qi - n) * tk, tk), tk)

    def scores(n, z_ref, sp_ref, diagonal=False):
        rows = key_rows(n)
        for g in range(heads):
            cols = slice(g * HEAD_DIM, (g + 1) * HEAD_DIM)
            z, sp = _attn_scores(q_ref[:, cols], k_ref[rows, cols], diagonal)
            z_ref[g * tq:(g + 1) * tq, :] = z
            sp_ref[g * tq:(g + 1) * tq, :] = sp

    def accumulate(n, z_ref, sp_ref):
        rows = key_rows(n)
        s_all = jnp.dot(sp_ref[...], negu_ref[...], preferred_element_type=F32)
        for g in range(heads):
            cols = slice(g * HEAD_DIM, (g + 1) * HEAD_DIM)
            tile = slice(g * tq, (g + 1) * tq)
            _attn_accumulate(z_ref[tile, :], s_all[tile, :], v_ref[rows, cols],
                             acc_ref, csum_ref, cols)

    scores(0, z0_ref, sp0_ref, diagonal=True)

    def pair(p, carry):
        n = 2 * p
        accumulate(n, z0_ref, sp0_ref)
        scores(n + 1, z1_ref, sp1_ref)
        accumulate(n + 1, z1_ref, sp1_ref)
        scores(n + 2, z0_ref, sp0_ref)
        return carry

    lax.fori_loop(0, qi // 2, pair, 0)
    n_done = 2 * (qi // 2)

    @pl.when(qi % 2 == 1)
    def _():
        accumulate(n_done, z0_ref, sp0_ref)
        scores(n_done + 1, z1_ref, sp1_ref)
        accumulate(n_done + 1, z1_ref, sp1_ref)
        convert_slabs()

    @pl.when(qi % 2 == 0)
    def _():
        accumulate(n_done, z0_ref, sp0_ref)
        convert_slabs()

    o_ref[...] = acc_ref[...].astype(o_ref.dtype)


def _attention(qkv, casts, batch, seq, n_heads, *, tq, heads):
    tk = tq
    nq = seq // tq
    width = heads * HEAD_DIM
    groups = n_heads // heads
    steps = batch * groups * nq
    row = lax.broadcasted_iota(jnp.int32, (tk, tk), 0)
    col = lax.broadcasted_iota(jnp.int32, (tk, tk), 1)
    negu = -(row >= col).astype(BF16)
    kernel = functools.partial(_attn_kernel, tq=tq, tk=tk, heads=heads, n_casts=len(casts))
    for w in casts:
        assert w.shape[0] % (steps * V7X_BF16_ROWS) == 0, w.shape
    slab_specs = [pl.BlockSpec((w.shape[0] // steps, w.shape[1]),
                               lambda b, h, i: ((b * groups + h) * nq + i, 0)) for w in casts]
    outs = pl.pallas_call(
        kernel,
        grid=(batch, groups, nq),
        in_specs=[
            pl.BlockSpec((tq, width), lambda b, h, i: (b * nq + i, h)),
            pl.BlockSpec((seq, width), lambda b, h, i: (b, groups + h)),
            pl.BlockSpec((seq, width), lambda b, h, i: (b, 2 * groups + h)),
            pl.BlockSpec((tk, tk), lambda b, h, i: (0, 0)),
        ] + slab_specs,
        out_specs=[pl.BlockSpec((tq, width), lambda b, h, i: (b * nq + i, h))] + slab_specs,
        out_shape=[jax.ShapeDtypeStruct((batch * seq, n_heads * HEAD_DIM), F32)]
        + [jax.ShapeDtypeStruct(w.shape, BF16) for w in casts],
        scratch_shapes=[
            pltpu.VMEM((tq, width), F32),
            pltpu.VMEM((tq, width), F32),
            pltpu.VMEM((heads * tq, tk), F32),
            pltpu.VMEM((heads * tq, tk), BF16),
            pltpu.VMEM((heads * tq, tk), F32),
            pltpu.VMEM((heads * tq, tk), BF16),
        ],
        compiler_params=_params(("parallel", "parallel", "arbitrary")),
        name="stickbreak_attention",
    )(qkv, qkv, qkv, negu, *casts)
    return outs[0], outs[1:]


RSQRT_FLOOR = 1e-30
SCAN_UNROLL = 4


def _softplus(x):
    return jnp.maximum(x, 0.0) + jnp.log1p(jnp.exp(-jnp.abs(x)))


def _sigmoid(x):
    return 0.5 * (jnp.tanh(0.5 * x) + 1.0)


def _rglru_kernel(xr_ref, yr_ref, wc_ref, bc_ref, wa_ref, ba_ref, wx_ref, bx_ref, lam_ref,
                  o_ref, xext_ref, a_ref, b_ref, h_ref, *, ts, tw):
    si = pl.program_id(2)

    @pl.when(si == 0)
    def _():
        xext_ref[pl.ds(0, HALO), :] = jnp.zeros((HALO, tw), F32)
        h_ref[...] = jnp.zeros_like(h_ref)

    @pl.when(si != 0)
    def _():
        xext_ref[pl.ds(0, HALO), :] = xext_ref[pl.ds(ts, HALO), :]

    xext_ref[pl.ds(HALO, ts), :] = xr_ref[...]

    xc = bc_ref[...] + wc_ref[pl.ds(REC_CONV - 1, 1), :] * xr_ref[...]
    for k in range(REC_CONV - 1):
        shift = REC_CONV - 1 - k
        xc = xc + wc_ref[pl.ds(k, 1), :] * xext_ref[pl.ds(HALO - shift, ts), :]

    neg_c_sp = -RG_C * _softplus(-lam_ref[...])
    xc16 = xc.astype(BF16)
    for n in range(tw // V7X_LANES):
        cols = slice(n * V7X_LANES, (n + 1) * V7X_LANES)
        xb = xc16[:, cols]
        r = _sigmoid(jnp.dot(xb, wa_ref[n], preferred_element_type=F32) + ba_ref[:, cols])
        i = _sigmoid(jnp.dot(xb, wx_ref[n], preferred_element_type=F32) + bx_ref[:, cols])
        log_a = neg_c_sp[:, cols] * r
        a = jnp.exp(log_a)
        u = (1.0 - a) * (1.0 + a)
        mult = u * lax.rsqrt(jnp.maximum(u, RSQRT_FLOOR))
        a_ref[:, cols] = a
        b_ref[:, cols] = mult * (i * xc[:, cols])

    row = lax.broadcasted_iota(jnp.int32, (V7X_SUBLANES, tw), 0)

    def group(gi, h_prev):
        r0 = pl.multiple_of(gi * V7X_SUBLANES, V7X_SUBLANES)
        a = a_ref[pl.ds(r0, V7X_SUBLANES), :]
        b = b_ref[pl.ds(r0, V7X_SUBLANES), :]
        for d in (1, 2, 4):
            keep = row >= d
            a_sh = jnp.where(keep, pltpu.roll(a, d, 0), 1.0)
            b_sh = jnp.where(keep, pltpu.roll(b, d, 0), 0.0)
            b = a * b_sh + b
            a = a * a_sh
        h = a * h_prev + b
        b_ref[pl.ds(r0, V7X_SUBLANES), :] = h
        return jnp.broadcast_to(h[V7X_SUBLANES - 1:, :], (V7X_SUBLANES, tw))

    h_last = lax.fori_loop(0, ts // V7X_SUBLANES, group, h_ref[...], unroll=SCAN_UNROLL)
    h_ref[...] = h_last
    o_ref[...] = jax.nn.gelu(yr_ref[...]) * b_ref[...]


def _rglru(xy, w_rconv, b_rconv, w_rg_a, b_rg_a, w_rg_x, b_rg_x, lam, batch, seq, *, ts, tw):
    width = lam.shape[-1]
    nw = width // tw
    ns = seq // ts
    gb = tw // V7X_LANES
    row_spec = pl.BlockSpec((1, tw), lambda b, w, s: (0, w))
    kernel = functools.partial(_rglru_kernel, ts=ts, tw=tw)
    return pl.pallas_call(
        kernel,
        grid=(batch, nw, ns),
        in_specs=[
            pl.BlockSpec((ts, tw), lambda b, w, s: (b * ns + s, w)),
            pl.BlockSpec((ts, tw), lambda b, w, s: (b * ns + s, nw + w)),
            pl.BlockSpec((REC_CONV, tw), lambda b, w, s: (0, w)),
            row_spec,
            pl.BlockSpec((gb, V7X_LANES, V7X_LANES), lambda b, w, s: (w, 0, 0)),
            row_spec,
            pl.BlockSpec((gb, V7X_LANES, V7X_LANES), lambda b, w, s: (w, 0, 0)),
            row_spec,
            row_spec,
        ],
        out_specs=pl.BlockSpec((ts, tw), lambda b, w, s: (b * ns + s, w)),
        out_shape=jax.ShapeDtypeStruct((batch * seq, width), F32),
        scratch_shapes=[
            pltpu.VMEM((ts + HALO, tw), F32),
            pltpu.VMEM((ts, tw), F32),
            pltpu.VMEM((ts, tw), F32),
            pltpu.VMEM((V7X_SUBLANES, tw), F32),
        ],
        compiler_params=_params(("parallel", "parallel", "arbitrary")),
        name="rglru_branch",
    )(xy, xy, w_rconv, b_rconv.reshape(1, width), w_rg_a.astype(BF16), b_rg_a.reshape(1, width),
      w_rg_x.astype(BF16), b_rg_x.reshape(1, width), lam.reshape(1, width))


def _outproj_kernel(att_hbm, rec_hbm, ga_ref, gr_ref, w_ref, res_ref, o_ref,
                    att_ref, rec_ref, a_ref, stat_ref, att_sem, rec_sem):
    def norms():
        _norm_into(a_ref, 0, 0, att_ref, ga_ref, att_ref.shape[0], stat_ref)
        _norm_into(a_ref, 0, att_ref.shape[1], rec_ref, gr_ref, rec_ref.shape[0], stat_ref)

    _row_tile_prologue([(att_hbm, att_ref, att_sem), (rec_hbm, rec_ref, rec_sem)], norms)
    o_ref[...] = res_ref[...] + jnp.dot(a_ref[...], w_ref[...], preferred_element_type=F32)


def _outproj(att, rec, g_att, g_rec, w, res, *, tm, tn):
    m, wa = att.shape
    wr = rec.shape[1]
    n = w.shape[1]
    return pl.pallas_call(
        _outproj_kernel,
        grid=(m // tm, n // tn),
        in_specs=[
            pl.BlockSpec(memory_space=pl.ANY),
            pl.BlockSpec(memory_space=pl.ANY),
            pl.BlockSpec((1, wa), lambda i, j: (0, 0)),
            pl.BlockSpec((1, wr), lambda i, j: (0, 0)),
            pl.BlockSpec((wa + wr, tn), lambda i, j: (0, j)),
            pl.BlockSpec((tm, tn), lambda i, j: (i, j)),
        ],
        out_specs=pl.BlockSpec((tm, tn), lambda i, j: (i, j)),
        out_shape=jax.ShapeDtypeStruct((m, n), F32),
        scratch_shapes=[
            pltpu.VMEM((tm, wa), F32),
            pltpu.VMEM((tm, wr), F32),
            pltpu.VMEM((tm, wa + wr), BF16),
            pltpu.VMEM((tm, V7X_LANES), F32),
            pltpu.SemaphoreType.DMA(()),
            pltpu.SemaphoreType.DMA(()),
        ],
        compiler_params=_params(("arbitrary", "arbitrary")),
        name="out_projection",
    )(att, rec, g_att, g_rec, w, res)


def _ffn_up_kernel(h_hbm, halo_ref, g_ref, wg_ref, wu_ref, cg_ref, cu_ref, bg_ref, bu_ref,
                   o_ref, h_ref, a_ref, stat_ref, sem, *, tm, seq):
    def norms():
        _norm_into(a_ref, FFN_HALO, 0, h_ref, g_ref, tm, stat_ref)
        seq_start = (pl.program_id(0) * tm) % seq == 0
        halo = _rms_rows(halo_ref[...], g_ref[...])
        a_ref[pl.ds(0, FFN_HALO), :] = jnp.where(seq_start, 0.0, halo).astype(a_ref.dtype)

    _row_tile_prologue([(h_hbm, h_ref, sem)], norms)
    a = a_ref[...]

    def conv(w_ref, c_ref, b_ref):
        y = jnp.dot(a, w_ref[...], preferred_element_type=F32)
        out = b_ref[...] + c_ref[pl.ds(FF_CONV - 1, 1), :] * y[FFN_HALO:, :]
        for k in range(FF_CONV - 1):
            shift = FF_CONV - 1 - k
            out = out + c_ref[pl.ds(k, 1), :] * y[FFN_HALO - shift:FFN_HALO - shift + tm, :]
        return out

    gate = conv(wg_ref, cg_ref, bg_ref)
    up = conv(wu_ref, cu_ref, bu_ref)
    o_ref[...] = (jax.nn.gelu(gate) * up).astype(o_ref.dtype)


def _ffn_up(h, g, w_up, w_conv, b_conv, seq, *, tm, tf):
    m, d = h.shape
    f = w_up.shape[1] // 2
    nf = f // tf
    kernel = functools.partial(_ffn_up_kernel, tm=tm, seq=seq)
    halo_blocks = tm // FFN_HALO
    b_conv = b_conv.reshape(1, 2 * f)
    return pl.pallas_call(
        kernel,
        grid=(m // tm, nf),
        in_specs=[
            pl.BlockSpec(memory_space=pl.ANY),
            pl.BlockSpec((FFN_HALO, d), lambda i, j: (jnp.maximum(i * halo_blocks - 1, 0), 0)),
            pl.BlockSpec((1, d), lambda i, j: (0, 0)),
            pl.BlockSpec((d, tf), lambda i, j: (0, j)),
            pl.BlockSpec((d, tf), lambda i, j: (0, nf + j)),
            pl.BlockSpec((FF_CONV, tf), lambda i, j: (0, j)),
            pl.BlockSpec((FF_CONV, tf), lambda i, j: (0, nf + j)),
            pl.BlockSpec((1, tf), lambda i, j: (0, j)),
            pl.BlockSpec((1, tf), lambda i, j: (0, nf + j)),
        ],
        out_specs=pl.BlockSpec((tm, tf), lambda i, j: (i, j)),
        out_shape=jax.ShapeDtypeStruct((m, f), BF16),
        scratch_shapes=[
            pltpu.VMEM((tm, d), F32),
            pltpu.VMEM((FFN_HALO + tm, d), BF16),
            pltpu.VMEM((tm, V7X_LANES), F32),
            pltpu.SemaphoreType.DMA(()),
        ],
        compiler_params=_params(("arbitrary", "arbitrary")),
        name="ffn_up_conv_gate",
    )(h, h, g, w_up, w_up, w_conv, w_conv, b_conv, b_conv)


def _matmul_res_kernel(a_ref, w_ref, res_ref, o_ref):
    @pl.when(pl.program_id(2) == 0)
    def _():
        o_ref[...] = res_ref[...]

    o_ref[...] += jnp.dot(a_ref[...], w_ref[...], preferred_element_type=F32)


def _matmul_res(a, w, res, *, tm, tn, tk):
    m, kdim = a.shape
    n = w.shape[1]
    return pl.pallas_call(
        _matmul_res_kernel,
        grid=(m // tm, n // tn, kdim // tk),
        in_specs=[
            pl.BlockSpec((tm, tk), lambda i, j, k: (i, k)),
            pl.BlockSpec((tk, tn), lambda i, j, k: (k, j)),
            pl.BlockSpec((tm, tn), lambda i, j, k: (i, j)),
        ],
        out_specs=pl.BlockSpec((tm, tn), lambda i, j, k: (i, j)),
        out_shape=jax.ShapeDtypeStruct((m, n), F32),
        compiler_params=_params(("parallel", "parallel", "arbitrary")),
        name="ffn_down_projection",
    )(a, w, res)


def _ple_kernel(h_hbm, g_ref, wg_ref, p_ref, wp_ref, res_ref, o_ref, h_ref, a_ref, stat_ref, sem):
    _row_tile_prologue(
        [(h_hbm, h_ref, sem)],
        lambda: _norm_into(a_ref, 0, 0, h_ref, g_ref, h_ref.shape[0], stat_ref))
    gate = _sigmoid(jnp.dot(a_ref[...], wg_ref[...], preferred_element_type=F32))
    emb = jnp.dot(p_ref[...].astype(BF16), wp_ref[...], preferred_element_type=F32)
    o_ref[...] = res_ref[...] + emb * gate


def _ple(h, g, w_gate, p, w_ple, *, tm, tn):
    m, d = h.shape
    n = w_gate.shape[1]
    pd = p.shape[1]
    return pl.pallas_call(
        _ple_kernel,
        grid=(m // tm, n // tn),
        in_specs=[
            pl.BlockSpec(memory_space=pl.ANY),
            pl.BlockSpec((1, d), lambda i, j: (0, 0)),
            pl.BlockSpec((d, tn), lambda i, j: (0, j)),
            pl.BlockSpec((tm, pd), lambda i, j: (i, 0)),
            pl.BlockSpec((pd, tn), lambda i, j: (0, j)),
            pl.BlockSpec((tm, tn), lambda i, j: (i, j)),
        ],
        out_specs=pl.BlockSpec((tm, tn), lambda i, j: (i, j)),
        out_shape=jax.ShapeDtypeStruct((m, n), F32),
        scratch_shapes=[
            pltpu.VMEM((tm, d), F32),
            pltpu.VMEM((tm, d), BF16),
            pltpu.VMEM((tm, V7X_LANES), F32),
            pltpu.SemaphoreType.DMA(()),
        ],
        compiler_params=_params(("arbitrary", "arbitrary")),
        name="ple_gate",
    )(h, g, w_gate, p, w_ple, h)


def _rmsnorm_kernel(x_ref, g_ref, o_ref):
    o_ref[...] = _rms_rows(x_ref[...], g_ref[...])


def _rmsnorm(x, g, *, tm):
    m, d = x.shape
    return pl.pallas_call(
        _rmsnorm_kernel,
        grid=(m // tm,),
        in_specs=[pl.BlockSpec((tm, d), lambda i: (i, 0)), pl.BlockSpec((1, d), lambda i: (0, 0))],
        out_specs=pl.BlockSpec((tm, d), lambda i: (i, 0)),
        out_shape=jax.ShapeDtypeStruct((m, d), F32),
        compiler_params=_params(("parallel",)),
        name="final_rmsnorm",
    )(x, g)


def kernel(x, p, g_mix, w_in, w_rconv, b_rconv, w_rg_a, b_rg_a, w_rg_x, b_rg_x, lam, g_att_out, g_rec_out, w_out, g_ffn, w_up, w_ffconv, b_ffconv, w_down, g_ple, w_ple, w_ple_gate, g_final):
    batch, seq, d_model = x.shape
    depth = w_in.shape[0]
    lru_width = lam.shape[-1]
    att_width = w_out.shape[1] - lru_width
    n_heads = att_width // HEAD_DIM
    m = batch * seq
    assert seq % ROW_TILE == 0 and seq % LRU_TIME_TILE == 0 and seq % ATTN_BLOCK == 0, seq
    assert n_heads % ATTN_HEADS == 0 and lru_width == N_LRU_BLOCKS * V7X_LANES, (n_heads, lru_width)
    assert x.dtype == F32 and w_in.dtype == F32, (x.dtype, w_in.dtype)

    h = x.reshape(m, d_model)
    for l in range(depth):
        w_in_l = w_in[l].astype(BF16)
        qkv_scale = jnp.concatenate(
            [jnp.full((1, att_width), 1.0 / math.sqrt(HEAD_DIM), F32),
             jnp.ones((1, 2 * att_width), F32)], axis=1)
        g_mix_l = g_mix[l].reshape(1, d_model)
        qkv = _norm_matmul(h, g_mix_l, w_in_l, 0, 3 * att_width, qkv_scale, BF16,
                           tm=ROW_TILE, tn=IN_PROJ_COLS, name="in_projection_qkv")
        xy = _norm_matmul(h, g_mix_l, w_in_l, 3 * att_width, 2 * lru_width,
                          jnp.ones((1, 2 * lru_width), F32), F32,
                          tm=ROW_TILE, tn=IN_PROJ_COLS, name="in_projection_lru")
        att, (w_out_l, w_up_l, w_down_l, w_gate_l) = _attention(
            qkv, [w_out[l], w_up[l], w_down[l], w_ple_gate[l]], batch, seq, n_heads,
            tq=ATTN_BLOCK, heads=ATTN_HEADS)
        rec = _rglru(xy, w_rconv[l], b_rconv[l], w_rg_a[l], b_rg_a[l], w_rg_x[l], b_rg_x[l],
                     lam[l], batch, seq, ts=LRU_TIME_TILE, tw=LRU_WIDTH_TILE)
        h = _outproj(att, rec, g_att_out[l].reshape(1, att_width),
                     g_rec_out[l].reshape(1, lru_width), w_out_l, h,
                     tm=ROW_TILE, tn=OUT_PROJ_COLS)
        act = _ffn_up(h, g_ffn[l].reshape(1, d_model), w_up_l, w_ffconv[l],
                      b_ffconv[l], seq, tm=ROW_TILE, tf=FFN_HIDDEN_COLS)
        h = _matmul_res(act, w_down_l, h, tm=ROW_TILE, tn=FFN_DOWN_COLS, tk=FFN_DOWN_DEPTH)
        h = _ple(h, g_ple[l].reshape(1, d_model), w_gate_l,
                 p[l].reshape(m, -1), w_ple[l].astype(BF16), tm=ROW_TILE, tn=PLE_COLS)
    out = _rmsnorm(h, g_final.reshape(1, d_model), tm=FINAL_NORM_ROWS)
    return out.reshape(batch, seq, d_model)
```

```python
import functools
import math

import jax
import jax.numpy as jnp
from jax import lax
from jax.experimental import pallas as pl
from jax.experimental.pallas import tpu as pltpu

F32 = jnp.float32
BF16 = jnp.bfloat16

EPS = 1e-6
HEAD_DIM = 128
N_LRU_BLOCKS = 16
RG_C = 8.0
REC_CONV = 4
FF_CONV = 3

V7X_LANES = 128
V7X_SUBLANES = 8
V7X_VMEM_LIMIT_BYTES = 60000 * 1024

V7X_BF16_ROWS = 2 * V7X_SUBLANES

HALO = V7X_SUBLANES
FFN_HALO = V7X_BF16_ROWS
NORM_ROWS = 256

ROW_TILE = 1024
IN_PROJ_COLS = 1024
OUT_PROJ_COLS = 512
FFN_HIDDEN_COLS = 512
FFN_DOWN_COLS = 1024
FFN_DOWN_DEPTH = 4096
PLE_COLS = 512
ATTN_BLOCK = 256
ATTN_HEADS = 4
LRU_TIME_TILE = 1024
LRU_WIDTH_TILE = 1024
FINAL_NORM_ROWS = 512


def _params(semantics, vmem_bytes=V7X_VMEM_LIMIT_BYTES):
    return pltpu.CompilerParams(dimension_semantics=semantics, vmem_limit_bytes=vmem_bytes)


def _rms_rows(x, g):
    ms = jnp.mean(x * x, axis=-1, keepdims=True)
    return x * lax.rsqrt(ms + EPS) * g


def _norm_into(dst_ref, dst_row0, dst_col0, src_ref, g_ref, rows, stat_ref):
    width = src_ref.shape[-1]
    lane_tiles = width // V7X_LANES
    chunk = min(NORM_ROWS, rows)

    def sum_squares(c, carry):
        r0 = pl.multiple_of(c * chunk, chunk)
        x = src_ref[pl.ds(r0, chunk), :]
        sq = x * x
        parts = [sq[:, k * V7X_LANES:(k + 1) * V7X_LANES] for k in range(lane_tiles)]
        while len(parts) > 1:
            parts = [a + b for a, b in zip(parts[0::2], parts[1::2])] + parts[len(parts) & ~1:]
        stat_ref[pl.ds(r0, chunk), :] = parts[0]
        return carry

    lax.fori_loop(0, rows // chunk, sum_squares, 0)
    ms = jnp.sum(stat_ref[pl.ds(0, rows), :], axis=-1, keepdims=True) * (1.0 / width)
    stat_ref[pl.ds(0, rows), :] = jnp.broadcast_to(lax.rsqrt(ms + EPS), (rows, V7X_LANES))

    def scale(c, carry):
        r0 = pl.multiple_of(c * chunk, chunk)
        rstd = jnp.concatenate([stat_ref[pl.ds(r0, chunk), :]] * lane_tiles, axis=1)
        y = src_ref[pl.ds(r0, chunk), :] * rstd * g_ref[...]
        dst_ref[pl.ds(dst_row0 + r0, chunk), pl.ds(dst_col0, width)] = y.astype(dst_ref.dtype)
        return carry

    lax.fori_loop(0, rows // chunk, scale, 0)


def _row_tile_prologue(sources, consume):
    i = pl.program_id(0)

    def copies(tile):
        return [pltpu.make_async_copy(hbm.at[pl.ds(tile * buf.shape[0], buf.shape[0]), :], buf, sem)
                for hbm, buf, sem in sources]

    @pl.when(pl.program_id(1) == 0)
    def _():
        @pl.when(i == 0)
        def _():
            for c in copies(0):
                c.start()

        for c in copies(i):
            c.wait()
        consume()

        @pl.when(i + 1 < pl.num_programs(0))
        def _():
            for c in copies(i + 1):
                c.start()


def _norm_matmul_kernel(x_hbm, g_ref, w_ref, cs_ref, o_ref, x_ref, a_ref, stat_ref, sem):
    _row_tile_prologue(
        [(x_hbm, x_ref, sem)],
        lambda: _norm_into(a_ref, 0, 0, x_ref, g_ref, x_ref.shape[0], stat_ref))
    y = jnp.dot(a_ref[...], w_ref[...], preferred_element_type=F32)
    o_ref[...] = (y * cs_ref[...]).astype(o_ref.dtype)


def _norm_matmul(x, g, w, col0, n, col_scale, out_dtype, *, tm, tn, name):
    m, d = x.shape
    j0 = col0 // tn
    return pl.pallas_call(
        _norm_matmul_kernel,
        grid=(m // tm, n // tn),
        in_specs=[
            pl.BlockSpec(memory_space=pl.ANY),
            pl.BlockSpec((1, d), lambda i, j: (0, 0)),
            pl.BlockSpec((d, tn), lambda i, j: (0, j0 + j)),
            pl.BlockSpec((1, tn), lambda i, j: (0, j)),
        ],
        out_specs=pl.BlockSpec((tm, tn), lambda i, j: (i, j)),
        out_shape=jax.ShapeDtypeStruct((m, n), out_dtype),
        scratch_shapes=[
            pltpu.VMEM((tm, d), F32),
            pltpu.VMEM((tm, d), BF16),
            pltpu.VMEM((tm, V7X_LANES), F32),
            pltpu.SemaphoreType.DMA(()),
        ],
        compiler_params=_params(("arbitrary", "arbitrary")),
        name=name,
    )(x, g, w, col_scale)


MASKED_SCORE = -1e30


def _attn_scores(q, k, diagonal):
    tq, tk = q.shape[0], k.shape[0]
    z = lax.dot_general(q, k, (((1,), (1,)), ((), ())), preferred_element_type=F32)
    sp = jnp.maximum(z, 0.0) + jnp.log(1.0 + jnp.exp(-jnp.abs(z)))
    if diagonal:
        row = lax.broadcasted_iota(jnp.int32, (tq, tk), 0)
        col = lax.broadcasted_iota(jnp.int32, (tq, tk), 1)
        mask = col < row
        sp = jnp.where(mask, sp, 0.0)
        z = jnp.where(mask, z, MASKED_SCORE)
    return z, sp.astype(BF16)


def _attn_accumulate(z, s, v, acc_ref, csum_ref, cols):
    tk = z.shape[1]
    csum = csum_ref[:, cols]
    a = jnp.exp(z + s + jnp.concatenate([csum] * (tk // V7X_LANES), axis=1))
    acc_ref[:, cols] += jnp.dot(a.astype(BF16), v, preferred_element_type=F32)
    csum_ref[:, cols] = csum + jnp.broadcast_to(s[:, :1], csum.shape)


def _attn_kernel(*refs, tq, tk, heads, n_casts):
    q_ref, k_ref, v_ref, negu_ref = refs[:4]
    cast_in = refs[4:4 + n_casts]
    o_ref = refs[4 + n_casts]
    cast_out = refs[5 + n_casts:5 + 2 * n_casts]
    acc_ref, csum_ref, z0_ref, sp0_ref, z1_ref, sp1_ref = refs[5 + 2 * n_casts:]
    qi = pl.program_id(2)
    acc_ref[...] = jnp.zeros_like(acc_ref)
    csum_ref[...] = jnp.zeros_like(csum_ref)

    def convert_slabs():
        for src, dst in zip(cast_in, cast_out):
            dst[...] = src[...].astype(dst.dtype)

    def key_rows(n):
        return pl.ds(pl.multiple_of((qi - n) * tk, tk), tk)

    def scores(n, z_ref, sp_ref, diagonal=False):
        rows = key_rows(n)
        for g in range(heads):
            cols = slice(g * HEAD_DIM, (g + 1) * HEAD_DIM)
            z, sp = _attn_scores(q_ref[:, cols], k_ref[rows, cols], diagonal)
            z_ref[g * tq:(g + 1) * tq, :] = z
            sp_ref[g * tq:(g + 1) * tq, :] = sp

    def accumulate(n, z_ref, sp_ref):
        rows = key_rows(n)
        s_all = jnp.dot(sp_ref[...], negu_ref[...], preferred_element_type=F32)
        for g in range(heads):
            cols = slice(g * HEAD_DIM, (g + 1) * HEAD_DIM)
            tile = slice(g * tq, (g + 1) * tq)
            _attn_accumulate(z_ref[tile, :], s_all[tile, :], v_ref[rows, cols],
                             acc_ref, csum_ref, cols)

    scores(0, z0_ref, sp0_ref, diagonal=True)

    def pair(p, carry):
        n = 2 * p
        accumulate(n, z0_ref, sp0_ref)
        scores(n + 1, z1_ref, sp1_ref)
        accumulate(n + 1, z1_ref, sp1_ref)
        scores(n + 2, z0_ref, sp0_ref)
        return carry

    lax.fori_loop(0, qi // 2, pair, 0)
    n_done = 2 * (qi // 2)

    @pl.when(qi % 2 == 1)
    def _():
        accumulate(n_done, z0_ref, sp0_ref)
        scores(n_done + 1, z1_ref, sp1_ref)
        accumulate(n_done + 1, z1_ref, sp1_ref)
        convert_slabs()

    @pl.when(qi % 2 == 0)
    def _():
        accumulate(n_done, z0_ref, sp0_ref)
        convert_slabs()

    o_ref[...] = acc_ref[...].astype(o_ref.dtype)


def _attention(qkv, casts, batch, seq, n_heads, *, tq, heads):
    tk = tq
    nq = seq // tq
    width = heads * HEAD_DIM
    groups = n_heads // heads
    steps = batch * groups * nq
    row = lax.broadcasted_iota(jnp.int32, (tk, tk), 0)
    col = lax.broadcasted_iota(jnp.int32, (tk, tk), 1)
    negu = -(row >= col).astype(BF16)
    kernel = functools.partial(_attn_kernel, tq=tq, tk=tk, heads=heads, n_casts=len(casts))
    for w in casts:
        assert w.shape[0] % (steps * V7X_BF16_ROWS) == 0, w.shape
    slab_specs = [pl.BlockSpec((w.shape[0] // steps, w.shape[1]),
                               lambda b, h, i: ((b * groups + h) * nq + i, 0)) for w in casts]
    outs = pl.pallas_call(
        kernel,
        grid=(batch, groups, nq),
        in_specs=[
            pl.BlockSpec((tq, width), lambda b, h, i: (b * nq + i, h)),
            pl.BlockSpec((seq, width), lambda b, h, i: (b, groups + h)),
            pl.BlockSpec((seq, width), lambda b, h, i: (b, 2 * groups + h)),
            pl.BlockSpec((tk, tk), lambda b, h, i: (0, 0)),
        ] + slab_specs,
        out_specs=[pl.BlockSpec((tq, width), lambda b, h, i: (b * nq + i, h))] + slab_specs,
        out_shape=[jax.ShapeDtypeStruct((batch * seq, n_heads * HEAD_DIM), F32)]
        + [jax.ShapeDtypeStruct(w.shape, BF16) for w in casts],
        scratch_shapes=[
            pltpu.VMEM((tq, width), F32),
            pltpu.VMEM((tq, width), F32),
            pltpu.VMEM((heads * tq, tk), F32),
            pltpu.VMEM((heads * tq, tk), BF16),
            pltpu.VMEM((heads * tq, tk), F32),
            pltpu.VMEM((heads * tq, tk), BF16),
        ],
        compiler_params=_params(("parallel", "parallel", "arbitrary")),
        name="stickbreak_attention",
    )(qkv, qkv, qkv, negu, *casts)
    return outs[0], outs[1:]


RSQRT_FLOOR = 1e-30
SCAN_UNROLL = 4


def _softplus(x):
    return jnp.maximum(x, 0.0) + jnp.log1p(jnp.exp(-jnp.abs(x)))


def _sigmoid(x):
    return 0.5 * (jnp.tanh(0.5 * x) + 1.0)


def _rglru_kernel(xr_ref, yr_ref, wc_ref, bc_ref, wa_ref, ba_ref, wx_ref, bx_ref, lam_ref,
                  o_ref, xext_ref, a_ref, b_ref, h_ref, *, ts, tw):
    si = pl.program_id(2)

    @pl.when(si == 0)
    def _():
        xext_ref[pl.ds(0, HALO), :] = jnp.zeros((HALO, tw), F32)
        h_ref[...] = jnp.zeros_like(h_ref)

    @pl.when(si != 0)
    def _():
        xext_ref[pl.ds(0, HALO), :] = xext_ref[pl.ds(ts, HALO), :]

    xext_ref[pl.ds(HALO, ts), :] = xr_ref[...]

    xc = bc_ref[...] + wc_ref[pl.ds(REC_CONV - 1, 1), :] * xr_ref[...]
    for k in range(REC_CONV - 1):
        shift = REC_CONV - 1 - k
        xc = xc + wc_ref[pl.ds(k, 1), :] * xext_ref[pl.ds(HALO - shift, ts), :]

    neg_c_sp = -RG_C * _softplus(-lam_ref[...])
    xc16 = xc.astype(BF16)
    for n in range(tw // V7X_LANES):
        cols = slice(n * V7X_LANES, (n + 1) * V7X_LANES)
        xb = xc16[:, cols]
        r = _sigmoid(jnp.dot(xb, wa_ref[n], preferred_element_type=F32) + ba_ref[:, cols])
        i = _sigmoid(jnp.dot(xb, wx_ref[n], preferred_element_type=F32) + bx_ref[:, cols])
        log_a = neg_c_sp[:, cols] * r
        a = jnp.exp(log_a)
        u = (1.0 - a) * (1.0 + a)
        mult = u * lax.rsqrt(jnp.maximum(u, RSQRT_FLOOR))
        a_ref[:, cols] = a
        b_ref[:, cols] = mult * (i * xc[:, cols])

    row = lax.broadcasted_iota(jnp.int32, (V7X_SUBLANES, tw), 0)

    def group(gi, h_prev):
        r0 = pl.multiple_of(gi * V7X_SUBLANES, V7X_SUBLANES)
        a = a_ref[pl.ds(r0, V7X_SUBLANES), :]
        b = b_ref[pl.ds(r0, V7X_SUBLANES), :]
        for d in (1, 2, 4):
            keep = row >= d
            a_sh = jnp.where(keep, pltpu.roll(a, d, 0), 1.0)
            b_sh = jnp.where(keep, pltpu.roll(b, d, 0), 0.0)
            b = a * b_sh + b
            a = a * a_sh
        h = a * h_prev + b
        b_ref[pl.ds(r0, V7X_SUBLANES), :] = h
        return jnp.broadcast_to(h[V7X_SUBLANES - 1:, :], (V7X_SUBLANES, tw))

    h_last = lax.fori_loop(0, ts // V7X_SUBLANES, group, h_ref[...], unroll=SCAN_UNROLL)
    h_ref[...] = h_last
    o_ref[...] = jax.nn.gelu(yr_ref[...]) * b_ref[...]


def _rglru(xy, w_rconv, b_rconv, w_rg_a, b_rg_a, w_rg_x, b_rg_x, lam, batch, seq, *, ts, tw):
    width = lam.shape[-1]
    nw = width // tw
    ns = seq // ts
    gb = tw // V7X_LANES
    row_spec = pl.BlockSpec((1, tw), lambda b, w, s: (0, w))
    kernel = functools.partial(_rglru_kernel, ts=ts, tw=tw)
    return pl.pallas_call(
        kernel,
        grid=(batch, nw, ns),
        in_specs=[
            pl.BlockSpec((ts, tw), lambda b, w, s: (b * ns + s, w)),
            pl.BlockSpec((ts, tw), lambda b, w, s: (b * ns + s, nw + w)),
            pl.BlockSpec((REC_CONV, tw), lambda b, w, s: (0, w)),
            row_spec,
            pl.BlockSpec((gb, V7X_LANES, V7X_LANES), lambda b, w, s: (w, 0, 0)),
            row_spec,
            pl.BlockSpec((gb, V7X_LANES, V7X_LANES), lambda b, w, s: (w, 0, 0)),
            row_spec,
            row_spec,
        ],
        out_specs=pl.BlockSpec((ts, tw), lambda b, w, s: (b * ns + s, w)),
        out_shape=jax.ShapeDtypeStruct((batch * seq, width), F32),
        scratch_shapes=[
            pltpu.VMEM((ts + HALO, tw), F32),
            pltpu.VMEM((ts, tw), F32),
            pltpu.VMEM((ts, tw), F32),
            pltpu.VMEM((V7X_SUBLANES, tw), F32),
        ],
        compiler_params=_params(("parallel", "parallel", "arbitrary")),
        name="rglru_branch",
    )(xy, xy, w_rconv, b_rconv.reshape(1, width), w_rg_a.astype(BF16), b_rg_a.reshape(1, width),
      w_rg_x.astype(BF16), b_rg_x.reshape(1, width), lam.reshape(1, width))


def _outproj_kernel(att_hbm, rec_hbm, ga_ref, gr_ref, w_ref, res_ref, o_ref,
                    att_ref, rec_ref, a_ref, stat_ref, att_sem, rec_sem):
    def norms():
        _norm_into(a_ref, 0, 0, att_ref, ga_ref, att_ref.shape[0], stat_ref)
        _norm_into(a_ref, 0, att_ref.shape[1], rec_ref, gr_ref, rec_ref.shape[0], stat_ref)

    _row_tile_prologue([(att_hbm, att_ref, att_sem), (rec_hbm, rec_ref, rec_sem)], norms)
    o_ref[...] = res_ref[...] + jnp.dot(a_ref[...], w_ref[...], preferred_element_type=F32)


def _outproj(att, rec, g_att, g_rec, w, res, *, tm, tn):
    m, wa = att.shape
    wr = rec.shape[1]
    n = w.shape[1]
    return pl.pallas_call(
        _outproj_kernel,
        grid=(m // tm, n // tn),
        in_specs=[
            pl.BlockSpec(memory_space=pl.ANY),
            pl.BlockSpec(memory_space=pl.ANY),
            pl.BlockSpec((1, wa), lambda i, j: (0, 0)),
            pl.BlockSpec((1, wr), lambda i, j: (0, 0)),
            pl.BlockSpec((wa + wr, tn), lambda i, j: (0, j)),
            pl.BlockSpec((tm, tn), lambda i, j: (i, j)),
        ],
        out_specs=pl.BlockSpec((tm, tn), lambda i, j: (i, j)),
        out_shape=jax.ShapeDtypeStruct((m, n), F32),
        scratch_shapes=[
            pltpu.VMEM((tm, wa), F32),
            pltpu.VMEM((tm, wr), F32),
            pltpu.VMEM((tm, wa + wr), BF16),
            pltpu.VMEM((tm, V7X_LANES), F32),
            pltpu.SemaphoreType.DMA(()),
            pltpu.SemaphoreType.DMA(()),
        ],
        compiler_params=_params(("arbitrary", "arbitrary")),
        name="out_projection",
    )(att, rec, g_att, g_rec, w, res)


def _ffn_up_kernel(h_hbm, halo_ref, g_ref, wg_ref, wu_ref, cg_ref, cu_ref, bg_ref, bu_ref,
                   o_ref, h_ref, a_ref, stat_ref, sem, *, tm, seq):
    def norms():
        _norm_into(a_ref, FFN_HALO, 0, h_ref, g_ref, tm, stat_ref)
        seq_start = (pl.program_id(0) * tm) % seq == 0
        halo = _rms_rows(halo_ref[...], g_ref[...])
        a_ref[pl.ds(0, FFN_HALO), :] = jnp.where(seq_start, 0.0, halo).astype(a_ref.dtype)

    _row_tile_prologue([(h_hbm, h_ref, sem)], norms)
    a = a_ref[...]

    def conv(w_ref, c_ref, b_ref):
        y = jnp.dot(a, w_ref[...], preferred_element_type=F32)
        out = b_ref[...] + c_ref[pl.ds(FF_CONV - 1, 1), :] * y[FFN_HALO:, :]
        for k in range(FF_CONV - 1):
            shift = FF_CONV - 1 - k
            out = out + c_ref[pl.ds(k, 1), :] * y[FFN_HALO - shift:FFN_HALO - shift + tm, :]
        return out

    gate = conv(wg_ref, cg_ref, bg_ref)
    up = conv(wu_ref, cu_ref, bu_ref)
    o_ref[...] = (jax.nn.gelu(gate) * up).astype(o_ref.dtype)


def _ffn_up(h, g, w_up, w_conv, b_conv, seq, *, tm, tf):
    m, d = h.shape
    f = w_up.shape[1] // 2
    nf = f // tf
    kernel = functools.partial(_ffn_up_kernel, tm=tm, seq=seq)
    halo_blocks = tm // FFN_HALO
    b_conv = b_conv.reshape(1, 2 * f)
    return pl.pallas_call(
        kernel,
        grid=(m // tm, nf),
        in_specs=[
            pl.BlockSpec(memory_space=pl.ANY),
            pl.BlockSpec((FFN_HALO, d), lambda i, j: (jnp.maximum(i * halo_blocks - 1, 0), 0)),
            pl.BlockSpec((1, d), lambda i, j: (0, 0)),
            pl.BlockSpec((d, tf), lambda i, j: (0, j)),
            pl.BlockSpec((d, tf), lambda i, j: (0, nf + j)),
            pl.BlockSpec((FF_CONV, tf), lambda i, j: (0, j)),
            pl.BlockSpec((FF_CONV, tf), lambda i, j: (0, nf + j)),
            pl.BlockSpec((1, tf), lambda i, j: (0, j)),
            pl.BlockSpec((1, tf), lambda i, j: (0, nf + j)),
        ],
        out_specs=pl.BlockSpec((tm, tf), lambda i, j: (i, j)),
        out_shape=jax.ShapeDtypeStruct((m, f), BF16),
        scratch_shapes=[
            pltpu.VMEM((tm, d), F32),
            pltpu.VMEM((FFN_HALO + tm, d), BF16),
            pltpu.VMEM((tm, V7X_LANES), F32),
            pltpu.SemaphoreType.DMA(()),
        ],
        compiler_params=_params(("arbitrary", "arbitrary")),
        name="ffn_up_conv_gate",
    )(h, h, g, w_up, w_up, w_conv, w_conv, b_conv, b_conv)


def _matmul_res_kernel(a_ref, w_ref, res_ref, o_ref):
    @pl.when(pl.program_id(2) == 0)
    def _():
        o_ref[...] = res_ref[...]

    o_ref[...] += jnp.dot(a_ref[...], w_ref[...], preferred_element_type=F32)


def _matmul_res(a, w, res, *, tm, tn, tk):
    m, kdim = a.shape
    n = w.shape[1]
    return pl.pallas_call(
        _matmul_res_kernel,
        grid=(m // tm, n // tn, kdim // tk),
        in_specs=[
            pl.BlockSpec((tm, tk), lambda i, j, k: (i, k)),
            pl.BlockSpec((tk, tn), lambda i, j, k: (k, j)),
            pl.BlockSpec((tm, tn), lambda i, j, k: (i, j)),
        ],
        out_specs=pl.BlockSpec((tm, tn), lambda i, j, k: (i, j)),
        out_shape=jax.ShapeDtypeStruct((m, n), F32),
        compiler_params=_params(("parallel", "parallel", "arbitrary")),
        name="ffn_down_projection",
    )(a, w, res)


def _ple_kernel(h_hbm, g_ref, wg_ref, p_ref, wp_ref, res_ref, o_ref, h_ref, a_ref, stat_ref, sem):
    _row_tile_prologue(
        [(h_hbm, h_ref, sem)],
        lambda: _norm_into(a_ref, 0, 0, h_ref, g_ref, h_ref.shape[0], stat_ref))
    gate = _sigmoid(jnp.dot(a_ref[...], wg_ref[...], preferred_element_type=F32))
    emb = jnp.dot(p_ref[...].astype(BF16), wp_ref[...], preferred_element_type=F32)
    o_ref[...] = res_ref[...] + emb * gate


def _ple(h, g, w_gate, p, w_ple, *, tm, tn):
    m, d = h.shape
    n = w_gate.shape[1]
    pd = p.shape[1]
    return pl.pallas_call(
        _ple_kernel,
        grid=(m // tm, n // tn),
        in_specs=[
            pl.BlockSpec(memory_space=pl.ANY),
            pl.BlockSpec((1, d), lambda i, j: (0, 0)),
            pl.BlockSpec((d, tn), lambda i, j: (0, j)),
            pl.BlockSpec((tm, pd), lambda i, j: (i, 0)),
            pl.BlockSpec((pd, tn), lambda i, j: (0, j)),
            pl.BlockSpec((tm, tn), lambda i, j: (i, j)),
        ],
        out_specs=pl.BlockSpec((tm, tn), lambda i, j: (i, j)),
        out_shape=jax.ShapeDtypeStruct((m, n), F32),
        scratch_shapes=[
            pltpu.VMEM((tm, d), F32),
            pltpu.VMEM((tm, d), BF16),
            pltpu.VMEM((tm, V7X_LANES), F32),
            pltpu.SemaphoreType.DMA(()),
        ],
        compiler_params=_params(("arbitrary", "arbitrary")),
        name="ple_gate",
    )(h, g, w_gate, p, w_ple, h)


def _rmsnorm_kernel(x_ref, g_ref, o_ref):
    o_ref[...] = _rms_rows(x_ref[...], g_ref[...])


def _rmsnorm(x, g, *, tm):
    m, d = x.shape
    return pl.pallas_call(
        _rmsnorm_kernel,
        grid=(m // tm,),
        in_specs=[pl.BlockSpec((tm, d), lambda i: (i, 0)), pl.BlockSpec((1, d), lambda i: (0, 0))],
        out_specs=pl.BlockSpec((tm, d), lambda i: (i, 0)),
        out_shape=jax.ShapeDtypeStruct((m, d), F32),
        compiler_params=_params(("parallel",)),
        name="final_rmsnorm",
    )(x, g)


def kernel(x, p, g_mix, w_in, w_rconv, b_rconv, w_rg_a, b_rg_a, w_rg_x, b_rg_x, lam, g_att_out, g_rec_out, w_out, g_ffn, w_up, w_ffconv, b_ffconv, w_down, g_ple, w_ple, w_ple_gate, g_final):
    batch, seq, d_model = x.shape
    depth = w_in.shape[0]
    lru_width = lam.shape[-1]
    att_width = w_out.shape[1] - lru_width
    n_heads = att_width // HEAD_DIM
    m = batch * seq
    assert seq % ROW_TILE == 0 and seq % LRU_TIME_TILE == 0 and seq % ATTN_BLOCK == 0, seq
    assert n_heads % ATTN_HEADS == 0 and lru_width == N_LRU_BLOCKS * V7X_LANES, (n_heads, lru_width)
    assert x.dtype == F32 and w_in.dtype == F32, (x.dtype, w_in.dtype)

    h = x.reshape(m, d_model)
    for l in range(depth):
        w_in_l = w_in[l].astype(BF16)
        qkv_scale = jnp.concatenate(
            [jnp.full((1, att_width), 1.0 / math.sqrt(HEAD_DIM), F32),
             jnp.ones((1, 2 * att_width), F32)], axis=1)
        g_mix_l = g_mix[l].reshape(1, d_model)
        qkv = _norm_matmul(h, g_mix_l, w_in_l, 0, 3 * att_width, qkv_scale, BF16,
                           tm=ROW_TILE, tn=IN_PROJ_COLS, name="in_projection_qkv")
        xy = _norm_matmul(h, g_mix_l, w_in_l, 3 * att_width, 2 * lru_width,
                          jnp.ones((1, 2 * lru_width), F32), F32,
                          tm=ROW_TILE, tn=IN_PROJ_COLS, name="in_projection_lru")
        att, (w_out_l, w_up_l, w_down_l, w_gate_l) = _attention(
            qkv, [w_out[l], w_up[l], w_down[l], w_ple_gate[l]], batch, seq, n_heads,
            tq=ATTN_BLOCK, heads=ATTN_HEADS)
        rec = _rglru(xy, w_rconv[l], b_rconv[l], w_rg_a[l], b_rg_a[l], w_rg_x[l], b_rg_x[l],
                     lam[l], batch, seq, ts=LRU_TIME_TILE, tw=LRU_WIDTH_TILE)
        h = _outproj(att, rec, g_att_out[l].reshape(1, att_width),
                     g_rec_out[l].reshape(1, lru_width), w_out_l, h,
                     tm=ROW_TILE, tn=OUT_PROJ_COLS)
        act = _ffn_up(h, g_ffn[l].reshape(1, d_model), w_up_l, w_ffconv[l],
                      b_ffconv[l], seq, tm=ROW_TILE, tf=FFN_HIDDEN_COLS)
        h = _matmul_res(act, w_down_l, h, tm=ROW_TILE, tn=FFN_DOWN_COLS, tk=FFN_DOWN_DEPTH)
        h = _ple(h, g_ple[l].reshape(1, d_model), w_gate_l,
                 p[l].reshape(m, -1), w_ple[l].astype(BF16), tm=ROW_TILE, tn=PLE_COLS)
    out = _rmsnorm(h, g_final.reshape(1, d_model), tm=FINAL_NORM_ROWS)
    return out.reshape(batch, seq, d_model)
```
